```python
import math
import jax
import jax.numpy as jnp
from jax import lax
import numpy as np

D_MODEL = 1024
BATCH = 8
SEQ = 4096
DEPTH = 2

GRID_W = 64
CTX_LEN = 256
EPS = 1e-6
N_MOD = 6
SSD_HEADS = 16
SSD_HEAD_DIM = 64
SSD_INNER = SSD_HEADS * SSD_HEAD_DIM
SSD_GROUPS = 2
SSD_REP = SSD_HEADS // SSD_GROUPS
SSD_STATE = 128
SSD_CONV = 5
SSD_CHUNK = 128
SSD_XBC = SSD_INNER + 2 * SSD_GROUPS * SSD_STATE
CONV_DIM = 768
CONV_WIDTH = 31
POOL_DIM = 768
POOL_WINDOWS = (2, 4, 8, 16)
POOL_GROUPS = len(POOL_WINDOWS)
POOL_GROUP_DIM = POOL_DIM // POOL_GROUPS
POOL_OUT_DIM = D_MODEL // POOL_GROUPS
N_BRANCH = 3
FFN_HIDDEN = -(-(8 * D_MODEL) // (3 * 256)) * 256
IN_WIDTHS = (SSD_INNER, SSD_XBC, 2 * SSD_HEADS, 2 * CONV_DIM, POOL_DIM, N_BRANCH * D_MODEL)
IN_DIM = sum(IN_WIDTHS)
SPLIT_POINTS = tuple(int(v) for v in np.cumsum(IN_WIDTHS)[:-1])

kernel_name = 'hybrid_ssd_conformer_pool_dit_block'


def rms_norm(x, g):
    xf = x.astype(jnp.float32)
    y = xf * lax.rsqrt(jnp.mean(xf * xf, axis=-1, keepdims=True) + EPS)
    return (y * g.astype(jnp.float32)).astype(x.dtype)


def layer_norm(x, g, b):
    xf = x.astype(jnp.float32)
    mu = jnp.mean(xf, axis=-1, keepdims=True)
    var = jnp.mean(jnp.square(xf - mu), axis=-1, keepdims=True)
    y = (xf - mu) * lax.rsqrt(var + EPS)
    return (y * g.astype(jnp.float32) + b.astype(jnp.float32)).astype(x.dtype)


def modulate(h, shift, scale):
    return h * (1 + scale) + shift


def depthwise_conv(x, w, b):
    k = w.shape[0]
    y = lax.conv_general_dilated(x, w[:, None, :].astype(x.dtype), window_strides=(1,),
                                 padding=[(k // 2, k - 1 - k // 2)],
                                 dimension_numbers=('NWC', 'WIO', 'NWC'),
                                 feature_group_count=x.shape[-1])
    return y + b


def centred_window_mean(x, window, axis):
    n = x.shape[axis]
    lo = window // 2
    hi = window - 1 - lo
    cs = jnp.cumsum(x.astype(jnp.float32), axis=axis)
    cs = jnp.concatenate([jnp.zeros_like(lax.slice_in_dim(cs, 0, 1, axis=axis)), cs], axis=axis)
    pos = np.arange(n)
    start = np.clip(pos - lo, 0, n)
    end = np.clip(pos + hi + 1, 0, n)
    total = jnp.take(cs, end, axis=axis) - jnp.take(cs, start, axis=axis)
    shape = [1] * x.ndim
    shape[axis] = n
    count = jnp.asarray((end - start).astype(np.float32).reshape(shape))
    return (total / count).astype(x.dtype)


def segsum(a):
    n = a.shape[-1]
    cs = jnp.cumsum(a, axis=-1)
    diff = cs[..., :, None] - cs[..., None, :]
    mask = np.tril(np.ones((n, n), dtype=bool))
    return jnp.where(mask, diff, -jnp.inf)


def ssd_chunked_scan(xs, da, bm, cm, init, need_y):
    nb, t = xs.shape[:2]
    nc = t // SSD_CHUNK
    xc = xs.reshape(nb, nc, SSD_CHUNK, SSD_GROUPS, SSD_REP, SSD_HEAD_DIM)
    bc = bm.reshape(nb, nc, SSD_CHUNK, SSD_GROUPS, SSD_STATE)
    cc = cm.reshape(nb, nc, SSD_CHUNK, SSD_GROUPS, SSD_STATE)
    a = da.reshape(nb, nc, SSD_CHUNK, SSD_GROUPS, SSD_REP).transpose(0, 3, 4, 1, 2)
    a_cs = jnp.cumsum(a, axis=-1)
    decay_to_end = jnp.exp(a_cs[..., -1:] - a_cs).transpose(0, 3, 4, 1, 2)
    states = jnp.einsum('bclgn,bclgrp->bcgrpn', bc, xc * decay_to_end[..., None])
    states = jnp.concatenate([init[:, None], states], axis=1)
    chunk_decay = jnp.exp(segsum(jnp.pad(a_cs[..., -1], ((0, 0), (0, 0), (0, 0), (1, 0)))))
    states = jnp.einsum('bgrzc,bcgrpn->bzgrpn', chunk_decay, states)
    prev, final = states[:, :-1], states[:, -1]
    if not need_y:
        return None, final
    decay_in = jnp.exp(segsum(a))
    cb = jnp.einsum('bclgn,bcsgn->bgcls', cc, bc)
    y_diag = jnp.einsum('bgrcls,bcsgrp->bclgrp', cb[:, :, None] * decay_in, xc)
    decay_from_start = jnp.exp(a_cs).transpose(0, 3, 4, 1, 2)
    y_off = jnp.einsum('bclgn,bcgrpn->bclgrp', cc, prev) * decay_from_start[..., None]
    y = (y_diag + y_off).reshape(nb, t, SSD_GROUPS, SSD_REP, SSD_HEAD_DIM)
    return y, final


def ssd_bidirectional(xbc, dt_raw, a_log, dt_bias, d_skip, inits, need_y):
    nb, t, _ = xbc.shape
    xbc32 = xbc.astype(jnp.float32)
    xh = xbc32[..., :SSD_INNER].reshape(nb, t, SSD_GROUPS, SSD_REP, SSD_HEAD_DIM)
    bm = xbc32[..., SSD_INNER:SSD_INNER + SSD_GROUPS * SSD_STATE].reshape(nb, t, SSD_GROUPS, SSD_STATE)
    cm = xbc32[..., SSD_INNER + SSD_GROUPS * SSD_STATE:].reshape(nb, t, SSD_GROUPS, SSD_STATE)
    dt = jax.nn.softplus(dt_raw.astype(jnp.float32).reshape(nb, t, 2, SSD_GROUPS, SSD_REP)
                         + dt_bias.astype(jnp.float32).reshape(2, SSD_GROUPS, SSD_REP))
    a = -jnp.exp(a_log.astype(jnp.float32)).reshape(2, SSD_GROUPS, SSD_REP)
    ys = []
    finals = []
    for d in range(2):
        dt_d = dt[:, :, d]
        args = (xh * dt_d[..., None], dt_d * a[d], bm, cm)
        if d == 1:
            args = tuple(jnp.flip(v, axis=1) for v in args)
        y_d, fin = ssd_chunked_scan(args[0], args[1], args[2], args[3], inits[d], need_y)
        finals.append(fin)
        if need_y:
            ys.append(y_d if d == 0 else jnp.flip(y_d, axis=1))
    if not need_y:
        return None, finals
    y = ys[0] + ys[1] + d_skip.astype(jnp.float32).reshape(SSD_GROUPS, SSD_REP, 1) * xh
    return y.reshape(nb, t, SSD_INNER).astype(xbc.dtype), finals


def pool_mixer(u, pool_w, pool_scale):
    groups = jnp.split(u, POOL_GROUPS, axis=-1)
    p = jnp.stack([centred_window_mean(g, w, 1) - g for g, w in zip(groups, POOL_WINDOWS)], axis=-2)
    y = jnp.einsum('...gc,gco->...go', p, pool_w)
    return y.reshape(y.shape[:-2] + (D_MODEL,)) * pool_scale


def token_mixer(h, rows, inits, need_out, w_in, ssd_conv_w, ssd_conv_b, ssd_a_log, ssd_dt_bias,
                ssd_d, ssd_norm_g, ssd_w_out, cv_dw_w, cv_dw_b, cv_ln_g, cv_ln_b, cv_w_out,
                pool_w, pool_scale, w_out):
    nb, t, _ = h.shape
    z, xbc, dt_raw, cv_in, pool_in, gate_logits = jnp.split(h @ w_in, SPLIT_POINTS, axis=-1)
    xbc = jax.nn.silu(depthwise_conv(xbc, ssd_conv_w, ssd_conv_b))
    y, finals = ssd_bidirectional(xbc, dt_raw, ssd_a_log, ssd_dt_bias, ssd_d, inits, need_out)
    if not need_out:
        return None, finals
    y = y * jax.nn.silu(z)
    y = rms_norm(y.reshape(nb, t, SSD_GROUPS, SSD_INNER // SSD_GROUPS),
                 ssd_norm_g.reshape(SSD_GROUPS, SSD_INNER // SSD_GROUPS)).reshape(nb, t, SSD_INNER)
    y_ssd = y @ ssd_w_out
    a, b = jnp.split(cv_in, 2, axis=-1)
    u = a * jax.nn.sigmoid(b)
    if rows is not None:
        u = u.reshape(nb * rows, GRID_W, CONV_DIM)
    u = depthwise_conv(u, cv_dw_w, cv_dw_b).reshape(nb, t, CONV_DIM)
    y_conv = jax.nn.silu(layer_norm(u, cv_ln_g, cv_ln_b)) @ cv_w_out
    if rows is not None:
        pool_in = pool_in.reshape(nb, rows, GRID_W, POOL_DIM)
    y_pool = pool_mixer(pool_in, pool_w, pool_scale).reshape(nb, t, D_MODEL)
    g = jax.nn.sigmoid(gate_logits.reshape(nb, t, N_BRANCH, D_MODEL))
    merged = g[:, :, 0] * y_ssd + g[:, :, 1] * y_conv + g[:, :, 2] * y_pool
    return merged @ w_out, finals


def swiglu(h, w_gate, w_up, w_down):
    return (jax.nn.silu(h @ w_gate) * (h @ w_up)) @ w_down


def setup_inputs(seed: int = 0) -> dict:
    key = jax.random.key(seed)
    keys = list(jax.random.split(key, 40))
    counter = [0]

    def nk():
        counter[0] += 1
        return keys[counter[0] - 1]

    def nrm(shape, scale):
        return scale * jax.random.normal(nk(), shape, jnp.float32)

    L = DEPTH
    D = D_MODEL
    dt0 = jnp.exp(jax.random.uniform(nk(), (L, 2, SSD_HEADS), jnp.float32, math.log(1e-3), math.log(1e-1)))
    return {
        'x': nrm((BATCH, SEQ, D), 1.0),
        'c': nrm((BATCH, D), 1.0),
        'ctx': nrm((BATCH, CTX_LEN, D), 1.0),
        'c_ctx': nrm((D,), 1.0),
        'w_ada': nrm((L, D, N_MOD * D), 0.5 * D ** -0.5),
        'b_ada': nrm((L, N_MOD * D), 0.02),
        'norm1_g': 1.0 + nrm((L, D), 0.02),
        'norm2_g': 1.0 + nrm((L, D), 0.02),
        'w_in': nrm((L, D, IN_DIM), D ** -0.5),
        'ssd_conv_w': nrm((L, SSD_CONV, SSD_XBC), SSD_CONV ** -0.5),
        'ssd_conv_b': nrm((L, SSD_XBC), 0.02),
        'ssd_a_log': jnp.log(jax.random.uniform(nk(), (L, 2, SSD_HEADS), jnp.float32, 1.0, 16.0)),
        'ssd_dt_bias': dt0 + jnp.log(-jnp.expm1(-dt0)),
        'ssd_d': 1.0 + nrm((L, SSD_HEADS), 0.1),
        'ssd_norm_g': 1.0 + nrm((L, SSD_INNER), 0.02),
        'ssd_w_out': nrm((L, SSD_INNER, D), SSD_INNER ** -0.5),
        'cv_dw_w': nrm((L, CONV_WIDTH, CONV_DIM), CONV_WIDTH ** -0.5),
        'cv_dw_b': nrm((L, CONV_DIM), 0.02),
        'cv_ln_g': 1.0 + nrm((L, CONV_DIM), 0.02),
        'cv_ln_b': nrm((L, CONV_DIM), 0.02),
        'cv_w_out': nrm((L, CONV_DIM, D), CONV_DIM ** -0.5),
        'pool_w': nrm((L, POOL_GROUPS, POOL_GROUP_DIM, POOL_OUT_DIM), POOL_GROUP_DIM ** -0.5),
        'pool_scale': 1.0 + nrm((L, D), 0.1),
        'w_out': nrm((L, D, D), D ** -0.5),
        'ffn_w_gate': nrm((L, D, FFN_HIDDEN), D ** -0.5),
        'ffn_w_up': nrm((L, D, FFN_HIDDEN), D ** -0.5),
        'ffn_w_down': nrm((L, FFN_HIDDEN, D), FFN_HIDDEN ** -0.5),
        'final_g': 1.0 + nrm((D,), 0.02),
    }


def reference(x, c, ctx, c_ctx, w_ada, b_ada, norm1_g, norm2_g, w_in, ssd_conv_w, ssd_conv_b,
              ssd_a_log, ssd_dt_bias, ssd_d, ssd_norm_g, ssd_w_out, cv_dw_w, cv_dw_b, cv_ln_g,
              cv_ln_b, cv_w_out, pool_w, pool_scale, w_out, ffn_w_gate, ffn_w_up, ffn_w_down,
              final_g):
    nb = x.shape[0]
    rows = x.shape[1] // GRID_W
    zero_state = jnp.zeros((nb, SSD_GROUPS, SSD_REP, SSD_HEAD_DIM, SSD_STATE), jnp.float32)
    xc = ctx
    for l in range(DEPTH):
        last = l == DEPTH - 1
        mix = dict(w_in=w_in[l], ssd_conv_w=ssd_conv_w[l], ssd_conv_b=ssd_conv_b[l],
                   ssd_a_log=ssd_a_log[l], ssd_dt_bias=ssd_dt_bias[l], ssd_d=ssd_d[l],
                   ssd_norm_g=ssd_norm_g[l], ssd_w_out=ssd_w_out[l], cv_dw_w=cv_dw_w[l],
                   cv_dw_b=cv_dw_b[l], cv_ln_g=cv_ln_g[l], cv_ln_b=cv_ln_b[l],
                   cv_w_out=cv_w_out[l], pool_w=pool_w[l], pool_scale=pool_scale[l],
                   w_out=w_out[l])
        mod_lat = jnp.split((jax.nn.silu(c) @ w_ada[l] + b_ada[l])[:, None, :], N_MOD, axis=-1)
        mod_ctx = jnp.split(jax.nn.silu(c_ctx) @ w_ada[l] + b_ada[l], N_MOD, axis=-1)
        h_ctx = modulate(rms_norm(xc, norm1_g[l]), mod_ctx[0], mod_ctx[1])
        out_ctx, ctx_states = token_mixer(h_ctx, None, (zero_state, zero_state), not last, **mix)
        h_lat = modulate(rms_norm(x, norm1_g[l]), mod_lat[0], mod_lat[1])
        out_lat, _ = token_mixer(h_lat, rows, ctx_states, True, **mix)
        x = x + mod_lat[2] * out_lat
        h_lat = modulate(rms_norm(x, norm2_g[l]), mod_lat[3], mod_lat[4])
        x = x + mod_lat[5] * swiglu(h_lat, ffn_w_gate[l], ffn_w_up[l], ffn_w_down[l])
        if not last:
            xc = xc + mod_ctx[2] * out_ctx
            h_ctx = modulate(rms_norm(xc, norm2_g[l]), mod_ctx[3], mod_ctx[4])
            xc = xc + mod_ctx[5] * swiglu(h_ctx, ffn_w_gate[l], ffn_w_up[l], ffn_w_down[l])
    return rms_norm(x, final_g)
```

```python
import functools

import numpy as np
import jax
import jax.numpy as jnp
from jax import lax
from jax.experimental import pallas as pl
from jax.experimental.pallas import tpu as pltpu

F32 = jnp.float32
BF16 = jnp.bfloat16

D_MODEL = 1024
GRID_W = 64
EPS = 1e-6
N_MOD = 6
SSD_HEADS = 16
SSD_HEAD_DIM = 64
SSD_INNER = SSD_HEADS * SSD_HEAD_DIM
SSD_GROUPS = 2
SSD_STATE = 128
SSD_CONV = 5
SSD_XBC = SSD_INNER + 2 * SSD_GROUPS * SSD_STATE
CONV_DIM = 768
CONV_WIDTH = 31
POOL_DIM = 768
POOL_WINDOWS = (2, 4, 8, 16)
POOL_GROUPS = len(POOL_WINDOWS)
POOL_GROUP_DIM = POOL_DIM // POOL_GROUPS
POOL_OUT_DIM = D_MODEL // POOL_GROUPS
FFN_HIDDEN = 2816

LANES = 128
CHUNK = 128
POOL_PAD = 256
POOL_PADDED = POOL_GROUPS * POOL_PAD
DT_PAD = 2 * LANES
HALO = 16
CV_PAD = 16
VMEM_LIMIT = 56 * 1024 * 1024


def _const_spec(shape):
    nd = len(shape)
    return pl.BlockSpec(shape, lambda *_: (0,) * nd, pipeline_mode=pl.Buffered(1))


def _params(n_parallel):
    return pltpu.CompilerParams(dimension_semantics=("arbitrary",) * n_parallel,
                                vmem_limit_bytes=VMEM_LIMIT)


def _sigmoid(v):
    return 1.0 / (1.0 + jnp.exp(-v))


def _silu(v):
    return v * _sigmoid(v)


def _dot(a, b):
    return jnp.dot(a, b, preferred_element_type=F32)


def _ada_body(c_ref, w_ref, b_ref, o_ref):
    s = _silu(c_ref[...])
    o_ref[0] = jnp.dot(s, w_ref[0], preferred_element_type=F32,
                       precision=lax.Precision.HIGHEST) + b_ref[0]


def _ada(cc, w_ada, b_ada):
    nl, d, nm = w_ada.shape
    r = cc.shape[0]
    tn = 1024
    return pl.pallas_call(
        _ada_body,
        out_shape=jax.ShapeDtypeStruct((nl, r, nm), F32),
        grid=(nl, nm // tn),
        in_specs=[pl.BlockSpec((r, d), lambda l, j: (0, 0)),
                  pl.BlockSpec((1, d, tn), lambda l, j: (l, 0, j)),
                  pl.BlockSpec((1, 1, tn), lambda l, j: (l, 0, j))],
        out_specs=pl.BlockSpec((1, r, tn), lambda l, j: (l, 0, j)),
        compiler_params=_params(2),
        name="adaln",
    )(cc, w_ada, b_ada.reshape(nl, 1, nm))


def _norm_mod(x, g, shift, scale):
    ms = jnp.mean(x * x, axis=-1, keepdims=True)
    h = (x * lax.rsqrt(ms + EPS)) * g
    return h * (1.0 + scale) + shift


def _inproj_body(*refs, n_seg, col_chunk):
    x_ref, g_ref, sh_ref, sc_ref = refs[:4]
    w_refs = refs[4:4 + n_seg]
    o_refs = refs[4 + n_seg:]
    hb = _norm_mod(x_ref[...], g_ref[...], sh_ref[0], sc_ref[0]).astype(BF16)
    for w_ref, o_ref in zip(w_refs, o_refs):
        n = w_ref.shape[1]
        step = min(col_chunk, n)
        for j in range(0, n, step):
            o_ref[:, j:j + step] = _dot(hb, w_ref[:, j:j + step]).astype(o_ref.dtype)


def _inproj(x, g, shift, scale, weights, out_dtypes, tm, tiles_per_mod):
    n, d = x.shape
    n_seg = len(weights)
    mod_spec = pl.BlockSpec((1, 1, d), lambda i: (i // tiles_per_mod, 0, 0))
    return pl.pallas_call(
        functools.partial(_inproj_body, n_seg=n_seg, col_chunk=512),
        out_shape=[jax.ShapeDtypeStruct((n, w.shape[1]), dt) for w, dt in zip(weights, out_dtypes)],
        grid=(n // tm,),
        in_specs=[pl.BlockSpec((tm, d), lambda i: (i, 0)), _const_spec((1, d)), mod_spec, mod_spec]
                 + [_const_spec(w.shape) for w in weights],
        out_specs=[pl.BlockSpec((tm, w.shape[1]), lambda i: (i, 0)) for w in weights],
        compiler_params=_params(1),
        name="inproj",
    )(x, g.reshape(1, d), shift, scale, *weights)


def _cumsum_lanes(v, fwd):
    n = v.shape[1]
    lane = lax.broadcasted_iota(jnp.int32, v.shape, 1)
    pre, suf = v, v
    k = 1
    while k < n:
        pre = pre + jnp.where(lane >= k, pltpu.roll(pre, k, axis=1), 0.0)
        suf = suf + jnp.where(lane < n - k, pltpu.roll(suf, n - k, axis=1), 0.0)
        k *= 2
    return jnp.where(fwd, pre, suf)


def _ssd_body(xm_ref, xp_ref, xn_ref, dt_ref, z_ref, cw_ref, cb_ref, alog_ref, dtb_ref, dsk_ref,
              ng_ref, init_ref, y_ref, st_ref, xc_all, yf_all, *, n_steps, ts):
    s = pl.program_id(1)
    fwd = s < n_steps
    c = jnp.where(fwd, s, 2 * n_steps - 1 - s)
    base = pl.multiple_of(c * ts, ts)
    n_sub = ts // CHUNK
    inner = SSD_INNER
    half_grp = SSD_GROUPS * SSD_STATE

    @pl.when(jnp.logical_or(s == 0, s == n_steps))
    def _():
        st_ref[...] = init_ref[...]

    @pl.when(fwd)
    def _():
        prev = jnp.where(c > 0, xp_ref[0].astype(F32), 0.0)
        nxt = jnp.where(c < n_steps - 1, xn_ref[0].astype(F32), 0.0)
        padded = jnp.concatenate([prev, xm_ref[0].astype(F32), nxt], axis=0)
        rows = ts + 2 * HALO
        acc = jnp.broadcast_to(cb_ref[...], (ts, SSD_XBC))
        for k in range(SSD_CONV):
            sh = (SSD_CONV // 2 - k) % rows
            rk = padded if sh == 0 else pltpu.roll(padded, sh, axis=0)
            acc = acc + rk[HALO:HALO + ts] * cw_ref[k:k + 1, :]
        xc = _silu(acc)
        xc_all[pl.ds(base, ts), :] = xc.astype(BF16)
        yf_all[pl.ds(base, ts), :] = xc[:, :inner] * dsk_ref[...]

    a_neg = -jnp.exp(alog_ref[0])
    dt_bias = dtb_ref[0]
    li = lax.broadcasted_iota(jnp.int32, (CHUNK, CHUNK), 0)
    si = lax.broadcasted_iota(jnp.int32, (CHUNK, CHUNK), 1)
    tri = jnp.where(fwd, li - si, si - li) >= 0
    first_half = si < SSD_HEAD_DIM
    first_half_row = first_half[0:1]

    def chunk_body(j, carry):
        jj = jnp.where(fwd, j, n_sub - 1 - j)
        loc = pl.multiple_of(jj * CHUNK, CHUNK)
        off = pl.multiple_of(base + loc, CHUNK)
        xc = xc_all[pl.ds(off, CHUNK), :]
        dtr = dt_ref[0, pl.ds(loc, CHUNK), :] + dt_bias
        dt = jnp.maximum(dtr, 0.0) + jnp.log(1.0 + jnp.exp(-jnp.abs(dtr)))
        a = dt * a_neg
        a_t = a.T[:SSD_HEADS]
        dt_t = dt.T[:SSD_HEADS]
        e_t = _cumsum_lanes(a_t, fwd)
        tot = jnp.sum(a_t, axis=1, keepdims=True)
        er_t = e_t - jnp.log(dt_t)
        w_t = jnp.exp(tot - er_t)
        etot = jnp.broadcast_to(jnp.exp(tot), (SSD_HEADS, CHUNK))
        e_pad = jnp.concatenate([e_t, jnp.zeros((CHUNK - SSD_HEADS, CHUNK), F32)], axis=0)
        e_col = e_pad.T
        ee_col = jnp.exp(e_col)
        for g in range(SSD_GROUPS):
            b_g = xc[:, inner + g * SSD_STATE: inner + (g + 1) * SSD_STATE]
            c_g = xc[:, inner + half_grp + g * SSD_STATE: inner + half_grp + (g + 1) * SSD_STATE]
            cb = lax.dot_general(c_g, b_g, (((1,), (1,)), ((), ())), preferred_element_type=F32)
            b_t = b_g.astype(F32).T
            c_f = c_g.astype(F32)
            for pr in range(SSD_HEADS // SSD_GROUPS // 2):
                pair = g * (SSD_HEADS // SSD_GROUPS // 2) + pr
                cols = slice(pair * LANES, (pair + 1) * LANES)
                xp = xc[:, cols]
                zero = jnp.zeros_like(xp)
                x1 = jnp.where(first_half, xp, zero)
                x2 = jnp.where(first_half, zero, xp)
                st = st_ref[0, 0, :, cols]
                sb = st.astype(BF16)
                s1 = jnp.where(first_half, sb, zero)
                s2 = jnp.where(first_half, zero, sb)
                lhs, bw = [], []
                for h in (2 * pair, 2 * pair + 1):
                    ecol = jnp.broadcast_to(e_col[:, h:h + 1], (CHUNK, CHUNK))
                    m = jnp.exp(jnp.where(tri, ecol - er_t[h:h + 1, :], -1e30)) * cb
                    cd = c_f * jnp.broadcast_to(ee_col[:, h:h + 1], (CHUNK, SSD_STATE))
                    lhs += [m.astype(BF16), cd.astype(BF16)]
                    bw.append((b_t * w_t[h:h + 1, :]).astype(BF16))
                y_pair = _dot(jnp.concatenate(lhs, axis=1), jnp.concatenate([x1, s1, x2, s2], axis=0))
                contrib = _dot(jnp.concatenate(bw, axis=1), jnp.concatenate([x1, x2], axis=0))
                erow = jnp.where(first_half_row, etot[2 * pair:2 * pair + 1, :], etot[2 * pair + 1:2 * pair + 2, :])
                st_ref[0, 0, :, cols] = st * erow + contrib
                yf_all[pl.ds(off, CHUNK), cols] = yf_all[pl.ds(off, CHUNK), cols] + y_pair

        @pl.when(jnp.logical_not(fwd))
        def _():
            y = yf_all[pl.ds(off, CHUNK), :] * _silu(z_ref[0, pl.ds(loc, CHUNK), :].astype(F32))
            gw = inner // SSD_GROUPS
            for g in range(SSD_GROUPS):
                yg = y[:, g * gw:(g + 1) * gw]
                ms = jnp.mean(yg * yg, axis=-1, keepdims=True)
                yn = (yg * lax.rsqrt(ms + EPS)) * ng_ref[:, g * gw:(g + 1) * gw]
                y_ref[0, pl.ds(loc, CHUNK), g * gw:(g + 1) * gw] = yn.astype(y_ref.dtype)

        return carry

    lax.fori_loop(0, n_sub, chunk_body, 0)


def _ssd(xbc, dt, z, conv_w, conv_b, a_log, dt_bias, d_skip, norm_g, init, ts):
    nb, t, _ = xbc.shape
    n_steps = t // ts
    hb = ts // HALO

    def tok(s):
        return jnp.minimum(s, n_steps - 1)

    def tok_out(s):
        return jnp.where(s < n_steps, n_steps - 1, 2 * n_steps - 1 - s)

    def tok_dt(s):
        return jnp.where(s < n_steps, s, 2 * n_steps - 1 - s)

    def phase(s):
        return jnp.where(s < n_steps, 0, 1)

    pad = lambda v: jnp.pad(v.astype(F32), ((0, 0), (0, LANES - v.shape[1])))
    a_log_p = pad(a_log).reshape(2, 1, LANES)
    dt_bias_p = pad(dt_bias).reshape(2, 1, LANES)
    d_full = jnp.repeat(d_skip.astype(F32), SSD_HEAD_DIM).reshape(1, SSD_INNER)
    conv_w_p = jnp.pad(conv_w.astype(F32), ((0, 8 - SSD_CONV), (0, 0)))

    return pl.pallas_call(
        functools.partial(_ssd_body, n_steps=n_steps, ts=ts),
        out_shape=[jax.ShapeDtypeStruct((nb, t, SSD_INNER), BF16),
                   jax.ShapeDtypeStruct((nb, 2, SSD_STATE, SSD_INNER), F32)],
        grid=(nb, 2 * n_steps),
        in_specs=[
            pl.BlockSpec((1, ts, SSD_XBC), lambda b, s: (b, tok(s), 0)),
            pl.BlockSpec((1, HALO, SSD_XBC), lambda b, s: (b, jnp.maximum(tok(s) * hb - 1, 0), 0)),
            pl.BlockSpec((1, HALO, SSD_XBC), lambda b, s: (b, jnp.minimum((tok(s) + 1) * hb, t // HALO - 1), 0)),
            pl.BlockSpec((1, ts, LANES), lambda b, s: (b, tok_dt(s), phase(s))),
            pl.BlockSpec((1, ts, SSD_INNER), lambda b, s: (b, tok_out(s), 0)),
            _const_spec((8, SSD_XBC)),
            _const_spec((1, SSD_XBC)),
            pl.BlockSpec((1, 1, LANES), lambda b, s: (phase(s), 0, 0)),
            pl.BlockSpec((1, 1, LANES), lambda b, s: (phase(s), 0, 0)),
            _const_spec((1, SSD_INNER)),
            _const_spec((1, SSD_INNER)),
            pl.BlockSpec((1, 1, SSD_STATE, SSD_INNER), lambda b, s: (b, phase(s), 0, 0)),
        ],
        out_specs=[
            pl.BlockSpec((1, ts, SSD_INNER), lambda b, s: (b, tok_out(s), 0)),
            pl.BlockSpec((1, 1, SSD_STATE, SSD_INNER), lambda b, s: (b, phase(s), 0, 0)),
        ],
        scratch_shapes=[pltpu.VMEM((t, SSD_XBC), BF16), pltpu.VMEM((t, SSD_INNER), F32)],
        compiler_params=_params(2),
        name="ssd_scan",
    )(xbc, xbc, xbc, dt, z, conv_w_p, conv_b.reshape(1, SSD_XBC).astype(F32), a_log_p, dt_bias_p,
      d_full, norm_g.reshape(1, SSD_INNER).astype(F32), init)


def _cv_body(x_ref, w_ref, b_ref, lg_ref, lb_ref, o_ref, pad_ref, *, width, n_rows):
    zeros = jnp.zeros((CV_PAD, CONV_DIM), F32)
    pad_ref[0:CV_PAD, :] = zeros
    pad_ref[CV_PAD + width:, :] = zeros
    n_q = (CONV_WIDTH + 8) // 8

    def row_body(r, carry):
        xr = x_ref[r]
        u = xr[:, :CONV_DIM].astype(F32) * _sigmoid(xr[:, CONV_DIM:].astype(F32))
        pad_ref[CV_PAD:CV_PAD + width, :] = u
        cols = []
        for lt in range(CONV_DIM // LANES):
            ls = slice(lt * LANES, (lt + 1) * LANES)
            acc = jnp.broadcast_to(b_ref[:, ls], (width, LANES))
            for rr in range(8):
                shifted = pad_ref[rr:rr + width + 8 * (n_q - 1), ls]
                for q in range(n_q):
                    k = 8 * q + rr - (CV_PAD - CONV_WIDTH // 2)
                    if 0 <= k < CONV_WIDTH:
                        acc = acc + shifted[8 * q:8 * q + width] * w_ref[k:k + 1, ls]
            cols.append(acc)
        v = jnp.concatenate(cols, axis=1)
        mu = jnp.mean(v, axis=-1, keepdims=True)
        dv = v - mu
        var = jnp.mean(dv * dv, axis=-1, keepdims=True)
        yn = dv * lax.rsqrt(var + EPS) * lg_ref[...] + lb_ref[...]
        o_ref[r] = _silu(yn).astype(o_ref.dtype)
        return carry

    lax.fori_loop(0, n_rows, row_body, 0)


def _conv_branch(cv, width, dw_w, dw_b, ln_g, ln_b):
    n = cv.shape[0]
    rows = n // width
    n_rows = max(1, min(rows, 1024 // width))
    x3 = cv.reshape(rows, width, 2 * CONV_DIM)
    out = pl.pallas_call(
        functools.partial(_cv_body, width=width, n_rows=n_rows),
        out_shape=jax.ShapeDtypeStruct((rows, width, CONV_DIM), BF16),
        grid=(rows // n_rows,),
        in_specs=[pl.BlockSpec((n_rows, width, 2 * CONV_DIM), lambda i: (i, 0, 0)),
                  _const_spec((32, CONV_DIM)), _const_spec((1, CONV_DIM)),
                  _const_spec((1, CONV_DIM)), _const_spec((1, CONV_DIM))],
        out_specs=pl.BlockSpec((n_rows, width, CONV_DIM), lambda i: (i, 0, 0)),
        scratch_shapes=[pltpu.VMEM((width + 2 * CV_PAD, CONV_DIM), F32)],
        compiler_params=_params(1),
        name="conv_branch",
    )(x3, jnp.pad(dw_w.astype(F32), ((0, 32 - CONV_WIDTH), (0, 0))), dw_b.reshape(1, CONV_DIM).astype(F32),
      ln_g.reshape(1, CONV_DIM).astype(F32), ln_b.reshape(1, CONV_DIM).astype(F32))
    return out.reshape(n, CONV_DIM)


def _pool_body(x_ref, ic_ref, o_ref, *, n):
    max_lo = max(POOL_WINDOWS) // 2
    for k, w in enumerate(POOL_WINDOWS):
        ls = slice(k * POOL_PAD, (k + 1) * POOL_PAD)
        x = x_ref[0, :, :, ls].astype(F32)
        zeros = jnp.zeros((max_lo,) + x.shape[1:], F32)
        run = jnp.concatenate([zeros, x, zeros], axis=0)
        span = 1
        while span < w:
            run = run[:-span] + run[span:]
            span *= 2
        lo = w // 2
        total = run[max_lo - lo:max_lo - lo + n]
        o_ref[0, :, :, ls] = (total * ic_ref[:, :, ls] - x).astype(o_ref.dtype)


def _pool_branch(pool_in, n, cb):
    nb, _, c, _ = pool_in.shape
    pos = np.arange(n)
    inv = np.zeros((n, 1, POOL_PADDED), np.float32)
    for k, w in enumerate(POOL_WINDOWS):
        lo = w // 2
        cnt = np.clip(pos + (w - 1 - lo) + 1, 0, n) - np.clip(pos - lo, 0, n)
        inv[:, 0, k * POOL_PAD:(k + 1) * POOL_PAD] = (1.0 / cnt)[:, None]
    return pl.pallas_call(
        functools.partial(_pool_body, n=n),
        out_shape=jax.ShapeDtypeStruct(pool_in.shape, BF16),
        grid=(nb, c // cb),
        in_specs=[pl.BlockSpec((1, n, cb, POOL_PADDED), lambda b, j: (b, 0, j, 0)),
                  _const_spec((n, 1, POOL_PADDED))],
        out_specs=pl.BlockSpec((1, n, cb, POOL_PADDED), lambda b, j: (b, 0, j, 0)),
        compiler_params=_params(2),
        name="pool_branch",
    )(pool_in, jnp.asarray(inv))


def _merge_body(x_ref, ys_ref, uc_ref, pp_ref, gt_ref, mg_ref, ps_ref, wss_ref, wcv_ref, wpl_ref, wo_ref, o_ref):
    d = D_MODEL
    y_ssd = _dot(ys_ref[...], wss_ref[...])
    y_cv = _dot(uc_ref[...], wcv_ref[...])
    y_pool = jnp.concatenate(
        [_dot(pp_ref[:, k * POOL_PAD:(k + 1) * POOL_PAD], wpl_ref[k]) for k in range(POOL_GROUPS)], axis=1)
    y_pool = y_pool * ps_ref[...]
    merged = (_sigmoid(gt_ref[:, 0:d].astype(F32)) * y_ssd
              + _sigmoid(gt_ref[:, d:2 * d].astype(F32)) * y_cv
              + _sigmoid(gt_ref[:, 2 * d:3 * d].astype(F32)) * y_pool)
    o_ref[...] = x_ref[...] + mg_ref[0] * _dot(merged.astype(BF16), wo_ref[...])


def _merge(x, ys, uc, pp, gates, mod_gate, pool_scale, w_ssd, w_cv, w_pool, w_out, tm, tiles_per_mod):
    n, d = x.shape
    tok = lambda w: pl.BlockSpec((tm, w), lambda i: (i, 0))
    return pl.pallas_call(
        _merge_body,
        out_shape=jax.ShapeDtypeStruct((n, d), F32),
        grid=(n // tm,),
        in_specs=[tok(d), tok(SSD_INNER), tok(CONV_DIM), tok(POOL_PADDED), tok(3 * d),
                  pl.BlockSpec((1, 1, d), lambda i: (i // tiles_per_mod, 0, 0)),
                  _const_spec((1, d)), _const_spec(w_ssd.shape), _const_spec(w_cv.shape),
                  _const_spec(w_pool.shape), _const_spec(w_out.shape)],
        out_specs=tok(d),
        compiler_params=_params(1),
        name="merge",
    )(x, ys, uc, pp, gates, mod_gate, pool_scale.reshape(1, d).astype(F32), w_ssd, w_cv, w_pool, w_out)


def _ffn_body(x_ref, g_ref, sh_ref, sc_ref, gt_ref, wg_ref, wu_ref, wd_ref, fg_ref, o_ref, *, final, hid_chunk):
    x = x_ref[...]
    hb = _norm_mod(x, g_ref[...], sh_ref[0], sc_ref[0]).astype(BF16)
    acc = jnp.zeros(x.shape, F32)
    for j in range(0, FFN_HIDDEN, hid_chunk):
        gate = _dot(hb, wg_ref[:, j:j + hid_chunk])
        up = _dot(hb, wu_ref[:, j:j + hid_chunk])
        acc = acc + _dot((_silu(gate) * up).astype(BF16), wd_ref[j:j + hid_chunk, :])
    out = x + gt_ref[0] * acc
    if final:
        ms = jnp.mean(out * out, axis=-1, keepdims=True)
        out = (out * lax.rsqrt(ms + EPS)) * fg_ref[...]
    o_ref[...] = out


def _ffn(x, g, shift, scale, gate, w_gate, w_up, w_down, final_g, final, tm, tiles_per_mod):
    n, d = x.shape
    mod_spec = pl.BlockSpec((1, 1, d), lambda i: (i // tiles_per_mod, 0, 0))
    return pl.pallas_call(
        functools.partial(_ffn_body, final=final, hid_chunk=FFN_HIDDEN // 2),
        out_shape=jax.ShapeDtypeStruct((n, d), F32),
        grid=(n // tm,),
        in_specs=[pl.BlockSpec((tm, d), lambda i: (i, 0)), _const_spec((1, d)), mod_spec, mod_spec, mod_spec,
                  _const_spec(w_gate.shape), _const_spec(w_up.shape), _const_spec(w_down.shape),
                  _const_spec((1, d))],
        out_specs=pl.BlockSpec((tm, d), lambda i: (i, 0)),
        compiler_params=_params(1),
        name="ffn",
    )(x, g.reshape(1, d).astype(F32), shift, scale, gate, w_gate, w_up, w_down,
      final_g.reshape(1, d).astype(F32))


def _split_w_in(w):
    o = np.cumsum([0, SSD_INNER, SSD_XBC, 2 * SSD_HEADS, 2 * CONV_DIM, POOL_DIM, 3 * D_MODEL])
    seg = [w[:, o[i]:o[i + 1]] for i in range(6)]
    z, xbc, dt, cv, pool, gates = seg
    dt = jnp.concatenate([jnp.pad(dt[:, :SSD_HEADS], ((0, 0), (0, LANES - SSD_HEADS))),
                          jnp.pad(dt[:, SSD_HEADS:], ((0, 0), (0, LANES - SSD_HEADS)))], axis=1)
    pool = jnp.pad(pool.reshape(-1, POOL_GROUPS, POOL_GROUP_DIM),
                   ((0, 0), (0, 0), (0, POOL_PAD - POOL_GROUP_DIM))).reshape(-1, POOL_PADDED)
    return [v.astype(BF16) for v in (z, xbc, dt, cv, pool, gates)]


def kernel(x, c, ctx, c_ctx, w_ada, b_ada, norm1_g, norm2_g, w_in, ssd_conv_w, ssd_conv_b, ssd_a_log,
           ssd_dt_bias, ssd_d, ssd_norm_g, ssd_w_out, cv_dw_w, cv_dw_b, cv_ln_g, cv_ln_b, cv_w_out, pool_w,
           pool_scale, w_out, ffn_w_gate, ffn_w_up, ffn_w_down, final_g):
    nb, t, d = x.shape
    tc = ctx.shape[1]
    depth = w_in.shape[0]
    rows = t // GRID_W

    n_cond = -(-(nb + 1) // 8) * 8
    cc = jnp.concatenate([c, c_ctx[None, :], jnp.zeros((n_cond - nb - 1, d), F32)], axis=0)
    mods = _ada(cc, w_ada, b_ada)

    tm_lat = min(512, t)
    tm_ctx = min(256, tc)
    ts_lat = min(256, t)
    ts_ctx = min(256, tc)
    cb_lat = min(16, GRID_W)

    xl = x.reshape(nb * t, d)
    xc = ctx.reshape(nb * tc, d)
    zero_state = jnp.zeros((nb, 2, SSD_STATE, SSD_INNER), F32)

    for l in range(depth):
        last = l == depth - 1
        w_z, w_xbc, w_dt, w_cv, w_pool, w_gates = _split_w_in(w_in[l])
        w_ssd_o = ssd_w_out[l].astype(BF16)
        w_cv_o = cv_w_out[l].astype(BF16)
        w_pool_o = jnp.pad(pool_w[l], ((0, 0), (0, POOL_PAD - POOL_GROUP_DIM), (0, 0))).astype(BF16)
        w_o = w_out[l].astype(BF16)
        w_fg, w_fu, w_fd = (v[l].astype(BF16) for v in (ffn_w_gate, ffn_w_up, ffn_w_down))
        mod_lat = [mods[l, :nb, k * d:(k + 1) * d].reshape(nb, 1, d) for k in range(N_MOD)]
        mod_ctx = [mods[l, nb:nb + 1, k * d:(k + 1) * d].reshape(1, 1, d) for k in range(N_MOD)]
        ssd_args = (ssd_conv_w[l], ssd_conv_b[l], ssd_a_log[l], ssd_dt_bias[l], ssd_d[l], ssd_norm_g[l])
        cv_args = (cv_dw_w[l], cv_dw_b[l], cv_ln_g[l], cv_ln_b[l])

        def mixer_tail(xs, z_y, cv, pool4, gates, width, pool_n, pool_cb, mod, tm, tpm):
            uc = _conv_branch(cv, width, *cv_args)
            pp = _pool_branch(pool4, pool_n, pool_cb).reshape(-1, POOL_PADDED)
            xs = _merge(xs, z_y, uc, pp, gates, mod[2], pool_scale[l], w_ssd_o, w_cv_o, w_pool_o, w_o, tm, tpm)
            return xs

        n_ctx_tiles = nb * tc // tm_ctx
        if last:
            xbc_c, dt_c = _inproj(xc, norm1_g[l], mod_ctx[0], mod_ctx[1], [w_xbc, w_dt], [BF16, F32],
                                  tm_ctx, n_ctx_tiles)
            z_c = jnp.zeros((nb, tc, SSD_INNER), BF16)
        else:
            z_c, xbc_c, dt_c, cv_c, pool_c, gates_c = _inproj(
                xc, norm1_g[l], mod_ctx[0], mod_ctx[1], [w_z, w_xbc, w_dt, w_cv, w_pool, w_gates],
                [BF16, BF16, F32, BF16, F32, BF16], tm_ctx, n_ctx_tiles)
            z_c = z_c.reshape(nb, tc, SSD_INNER)
        y_c, ctx_state = _ssd(xbc_c.reshape(nb, tc, SSD_XBC), dt_c.reshape(nb, tc, DT_PAD), z_c,
                              *ssd_args, zero_state, ts_ctx)
        if not last:
            xc = mixer_tail(xc, y_c.reshape(nb * tc, SSD_INNER), cv_c, pool_c.reshape(nb, tc, 1, POOL_PADDED),
                            gates_c, tc, tc, 1, mod_ctx, tm_ctx, n_ctx_tiles)
            xc = _ffn(xc, norm2_g[l], mod_ctx[3], mod_ctx[4], mod_ctx[5], w_fg, w_fu, w_fd, final_g, False,
                      tm_ctx, n_ctx_tiles)

        tpm = t // tm_lat
        z_l, xbc_l, dt_l, cv_l, pool_l, gates_l = _inproj(
            xl, norm1_g[l], mod_lat[0], mod_lat[1], [w_z, w_xbc, w_dt, w_cv, w_pool, w_gates],
            [BF16, BF16, F32, BF16, BF16, BF16], tm_lat, tpm)
        y_l, _ = _ssd(xbc_l.reshape(nb, t, SSD_XBC), dt_l.reshape(nb, t, DT_PAD), z_l.reshape(nb, t, SSD_INNER),
                      *ssd_args, ctx_state, ts_lat)
        xl = mixer_tail(xl, y_l.reshape(nb * t, SSD_INNER), cv_l, pool_l.reshape(nb, rows, GRID_W, POOL_PADDED),
                        gates_l, GRID_W, rows, cb_lat, mod_lat, tm_lat, tpm)
        xl = _ffn(xl, norm2_g[l], mod_lat[3], mod_lat[4], mod_lat[5], w_fg, w_fu, w_fd, final_g, last,
                  tm_lat, tpm)

    return xl.reshape(nb, t, d)
```

```python
import functools

import numpy as np
import jax
import jax.numpy as jnp
from jax import lax
from jax.experimental import pallas as pl
from jax.experimental.pallas import tpu as pltpu

F32 = jnp.float32
BF16 = jnp.bfloat16

D_MODEL = 1024
GRID_W = 64
EPS = 1e-6
N_MOD = 6
SSD_HEADS = 16
SSD_HEAD_DIM = 64
SSD_INNER = SSD_HEADS * SSD_HEAD_DIM
SSD_GROUPS = 2
SSD_STATE = 128
SSD_CONV = 5
SSD_XBC = SSD_INNER + 2 * SSD_GROUPS * SSD_STATE
CONV_DIM = 768
CONV_WIDTH = 31
POOL_DIM = 768
POOL_WINDOWS = (2, 4, 8, 16)
POOL_GROUPS = len(POOL_WINDOWS)
POOL_GROUP_DIM = POOL_DIM // POOL_GROUPS
POOL_OUT_DIM = D_MODEL // POOL_GROUPS
FFN_HIDDEN = 2816

LANES = 128
CHUNK = 128
LOG2E = 1.4426950408889634
POOL_PAD = 256
POOL_PADDED = POOL_GROUPS * POOL_PAD
DT_PAD = 2 * LANES
HALO = 16
CV_PAD = 16
VMEM_LIMIT = 56 * 1024 * 1024


def _const_spec(shape):
    nd = len(shape)
    return pl.BlockSpec(shape, lambda *_: (0,) * nd, pipeline_mode=pl.Buffered(1))


def _params(n_parallel):
    return pltpu.CompilerParams(dimension_semantics=("arbitrary",) * n_parallel,
                                vmem_limit_bytes=VMEM_LIMIT)


def _sigmoid(v):
    return 1.0 / (1.0 + jnp.exp(-v))


def _silu(v):
    return v * _sigmoid(v)


def _dot(a, b):
    return jnp.dot(a, b, preferred_element_type=F32)


def _ada_body(c_ref, w_ref, b_ref, o_ref):
    s = _silu(c_ref[...])
    o_ref[0] = jnp.dot(s, w_ref[0], preferred_element_type=F32,
                       precision=lax.Precision.HIGHEST) + b_ref[0]


def _ada(cc, w_ada, b_ada):
    nl, d, nm = w_ada.shape
    r = cc.shape[0]
    tn = 1024
    return pl.pallas_call(
        _ada_body,
        out_shape=jax.ShapeDtypeStruct((nl, r, nm), F32),
        grid=(nl, nm // tn),
        in_specs=[pl.BlockSpec((r, d), lambda l, j: (0, 0)),
                  pl.BlockSpec((1, d, tn), lambda l, j: (l, 0, j)),
                  pl.BlockSpec((1, 1, tn), lambda l, j: (l, 0, j))],
        out_specs=pl.BlockSpec((1, r, tn), lambda l, j: (l, 0, j)),
        compiler_params=_params(2),
        name="adaln",
    )(cc, w_ada, b_ada.reshape(nl, 1, nm))


def _norm_mod(x, g, shift, scale):
    ms = jnp.mean(x * x, axis=-1, keepdims=True)
    h = (x * lax.rsqrt(ms + EPS)) * g
    return h * (1.0 + scale) + shift


def _inproj_body(*refs, n_seg, col_chunk, transposed):
    x_ref, g_ref, sh_ref, sc_ref = refs[:4]
    w_refs = refs[4:4 + n_seg]
    o_refs = refs[4 + n_seg:]
    hb = _norm_mod(x_ref[...], g_ref[...], sh_ref[0], sc_ref[0]).astype(BF16)
    for w_ref, o_ref, tr in zip(w_refs, o_refs, transposed):
        n = w_ref.shape[0] if tr else w_ref.shape[1]
        step = min(col_chunk, n)
        for j in range(0, n, step):
            if tr:
                o_ref[j:j + step, :] = lax.dot_general(
                    w_ref[j:j + step, :], hb, (((1,), (1,)), ((), ())),
                    preferred_element_type=F32).astype(o_ref.dtype)
            else:
                o_ref[:, j:j + step] = _dot(hb, w_ref[:, j:j + step]).astype(o_ref.dtype)


def _inproj(x, g, shift, scale, weights, out_dtypes, tm, tiles_per_mod, transposed=None):
    n, d = x.shape
    n_seg = len(weights)
    transposed = tuple(transposed or (False,) * n_seg)
    mod_spec = pl.BlockSpec((1, 1, d), lambda i: (i // tiles_per_mod, 0, 0))
    out_shape, out_specs = [], []
    for w, dt, tr in zip(weights, out_dtypes, transposed):
        if tr:
            out_shape.append(jax.ShapeDtypeStruct((w.shape[0], n), dt))
            out_specs.append(pl.BlockSpec((w.shape[0], tm), lambda i: (0, i)))
        else:
            out_shape.append(jax.ShapeDtypeStruct((n, w.shape[1]), dt))
            out_specs.append(pl.BlockSpec((tm, w.shape[1]), lambda i: (i, 0)))
    return pl.pallas_call(
        functools.partial(_inproj_body, n_seg=n_seg, col_chunk=512, transposed=transposed),
        out_shape=out_shape,
        grid=(n // tm,),
        in_specs=[pl.BlockSpec((tm, d), lambda i: (i, 0)), _const_spec((1, d)), mod_spec, mod_spec]
                 + [_const_spec(w.shape) for w in weights],
        out_specs=out_specs,
        compiler_params=_params(1),
        name="inproj",
    )(x, g.reshape(1, d), shift, scale, *weights)


def _cumsum_lanes(v, fwd):
    n = v.shape[1]
    lane = lax.broadcasted_iota(jnp.int32, v.shape, 1)
    pre, suf = v, v
    k = 1
    while k < n:
        pre = pre + jnp.where(lane >= k, pltpu.roll(pre, k, axis=1), 0.0)
        suf = suf + jnp.where(lane < n - k, pltpu.roll(suf, n - k, axis=1), 0.0)
        k *= 2
    return jnp.where(fwd, pre, suf)


def _ssd_body(xm_ref, xp_ref, xn_ref, dt_ref, z_ref, cw_ref, cb_ref, alog_ref, dtb_ref, dsk_ref,
              ng_ref, init_ref, y_ref, st_ref, xc_all, yf_all, pt_ref, ec_ref, *, n_steps, ts):
    s = pl.program_id(1)
    fwd = s < n_steps
    c = jnp.where(fwd, s, 2 * n_steps - 1 - s)
    base = pl.multiple_of(c * ts, ts)
    n_sub = ts // CHUNK
    nh = SSD_HEADS
    inner = SSD_INNER
    half_grp = SSD_GROUPS * SSD_STATE
    pairs_per_group = SSD_HEADS // SSD_GROUPS // 2

    @pl.when(jnp.logical_or(s == 0, s == n_steps))
    def _():
        st_ref[...] = init_ref[...]

    @pl.when(fwd)
    def _():
        prev = jnp.where(c > 0, xp_ref[0].astype(F32), 0.0)
        nxt = jnp.where(c < n_steps - 1, xn_ref[0].astype(F32), 0.0)
        padded = jnp.concatenate([prev, xm_ref[0].astype(F32), nxt], axis=0)
        rows = ts + 2 * HALO
        acc = jnp.broadcast_to(cb_ref[...], (ts, SSD_XBC))
        for k in range(SSD_CONV):
            sh = (SSD_CONV // 2 - k) % rows
            rk = padded if sh == 0 else pltpu.roll(padded, sh, axis=0)
            acc = acc + rk[HALO:HALO + ts] * cw_ref[k:k + 1, :]
        xc = _silu(acc)
        xc_all[pl.ds(base, ts), :] = xc.astype(BF16)
        yf_all[pl.ds(base, ts), :] = xc[:, :inner] * dsk_ref[...]

    dtr = dt_ref[0]
    dtr_t = jnp.concatenate([dtr[j * CHUNK:(j + 1) * CHUNK].T[:nh] for j in range(n_sub)], axis=0)
    dtr_t = dtr_t + jnp.tile(dtb_ref[0], (n_sub, 1))
    dt_t = jnp.maximum(dtr_t, 0.0) + jnp.log(1.0 + jnp.exp(-jnp.abs(dtr_t)))
    a2_t = dt_t * jnp.tile(-jnp.exp(alog_ref[0]) * LOG2E, (n_sub, 1))
    e2_t = _cumsum_lanes(a2_t, fwd)
    tot2 = jnp.sum(a2_t, axis=1, keepdims=True)
    er2_t = e2_t - jnp.log(dt_t) * LOG2E
    pt_ref[0] = er2_t
    pt_ref[1] = jnp.exp2(tot2 - er2_t)
    pt_ref[2] = jnp.broadcast_to(jnp.exp2(tot2), e2_t.shape)
    for j in range(n_sub):
        e_pad = jnp.concatenate([e2_t[j * nh:(j + 1) * nh], jnp.zeros((CHUNK - nh, CHUNK), F32)], axis=0)
        ec_ref[j * CHUNK:(j + 1) * CHUNK, :] = e_pad.T

    li = lax.broadcasted_iota(jnp.int32, (CHUNK, CHUNK), 0)
    si = lax.broadcasted_iota(jnp.int32, (CHUNK, CHUNK), 1)
    tri = jnp.where(fwd, li - si, si - li) >= 0
    first_half = si < SSD_HEAD_DIM
    first_half_row = first_half[0:1]

    for j in range(n_sub):
        jj = jnp.where(fwd, j, n_sub - 1 - j)
        loc = pl.multiple_of(jj * CHUNK, CHUNK)
        off = pl.multiple_of(base + loc, CHUNK)
        hrow = pl.multiple_of(jj * nh, nh)
        xc = xc_all[pl.ds(off, CHUNK), :]
        er2 = pt_ref[0, pl.ds(hrow, nh), :]
        w_t = pt_ref[1, pl.ds(hrow, nh), :]
        etot = pt_ref[2, pl.ds(hrow, nh), :]
        e_col = ec_ref[pl.ds(loc, CHUNK), :]
        y_pairs = []
        for g in range(SSD_GROUPS):
            b_g = xc[:, inner + g * SSD_STATE: inner + (g + 1) * SSD_STATE]
            c_g = xc[:, inner + half_grp + g * SSD_STATE: inner + half_grp + (g + 1) * SSD_STATE]
            cb = lax.dot_general(c_g, b_g, (((1,), (1,)), ((), ())), preferred_element_type=F32)
            b_t = b_g.astype(F32).T
            c_f = c_g.astype(F32)
            for pr in range(pairs_per_group):
                pair = g * pairs_per_group + pr
                cols = slice(pair * LANES, (pair + 1) * LANES)
                xp = xc[:, cols]
                zero = jnp.zeros_like(xp)
                x1 = jnp.where(first_half, xp, zero)
                x2 = jnp.where(first_half, zero, xp)
                st = st_ref[0, 0, :, cols]
                sb = st.astype(BF16)
                s1 = jnp.where(first_half, sb, zero)
                s2 = jnp.where(first_half, zero, sb)
                lhs, bw = [], []
                for h in (2 * pair, 2 * pair + 1):
                    ecol = jnp.broadcast_to(e_col[:, h:h + 1], (CHUNK, CHUNK))
                    m = jnp.exp2(jnp.where(tri, ecol - er2[h:h + 1, :], -1e30)) * cb
                    cd = c_f * jnp.exp2(ecol)
                    lhs += [m.astype(BF16), cd.astype(BF16)]
                    bw.append((b_t * w_t[h:h + 1, :]).astype(BF16))
                y_pairs.append(_dot(jnp.concatenate(lhs, axis=1), jnp.concatenate([x1, s1, x2, s2], axis=0)))
                contrib = _dot(jnp.concatenate(bw, axis=1), jnp.concatenate([x1, x2], axis=0))
                erow = jnp.where(first_half_row, etot[2 * pair:2 * pair + 1, :], etot[2 * pair + 1:2 * pair + 2, :])
                st_ref[0, 0, :, cols] = st * erow + contrib
        yf_all[pl.ds(off, CHUNK), :] = yf_all[pl.ds(off, CHUNK), :] + jnp.concatenate(y_pairs, axis=1)

    @pl.when(jnp.logical_not(fwd))
    def _():
        y = yf_all[pl.ds(base, ts), :] * _silu(z_ref[0].astype(F32))
        gw = inner // SSD_GROUPS
        for g in range(SSD_GROUPS):
            yg = y[:, g * gw:(g + 1) * gw]
            ms = jnp.mean(yg * yg, axis=-1, keepdims=True)
            yn = (yg * lax.rsqrt(ms + EPS)) * ng_ref[:, g * gw:(g + 1) * gw]
            y_ref[0, :, g * gw:(g + 1) * gw] = yn.astype(y_ref.dtype)


def _ssd(xbc, dt, z, conv_w, conv_b, a_log, dt_bias, d_skip, norm_g, init, ts):
    nb, t, _ = xbc.shape
    n_steps = t // ts
    hb = ts // HALO

    def tok(s):
        return jnp.minimum(s, n_steps - 1)

    def tok_out(s):
        return jnp.where(s < n_steps, n_steps - 1, 2 * n_steps - 1 - s)

    def tok_dt(s):
        return jnp.where(s < n_steps, s, 2 * n_steps - 1 - s)

    def phase(s):
        return jnp.where(s < n_steps, 0, 1)

    rep = lambda v: jnp.broadcast_to(v.astype(F32)[:, :, None], (2, SSD_HEADS, LANES))
    a_log_p = rep(a_log)
    dt_bias_p = rep(dt_bias)
    d_full = jnp.repeat(d_skip.astype(F32), SSD_HEAD_DIM).reshape(1, SSD_INNER)
    conv_w_p = jnp.pad(conv_w.astype(F32), ((0, 8 - SSD_CONV), (0, 0)))

    return pl.pallas_call(
        functools.partial(_ssd_body, n_steps=n_steps, ts=ts),
        out_shape=[jax.ShapeDtypeStruct((nb, t, SSD_INNER), BF16),
                   jax.ShapeDtypeStruct((nb, 2, SSD_STATE, SSD_INNER), F32)],
        grid=(nb, 2 * n_steps),
        in_specs=[
            pl.BlockSpec((1, ts, SSD_XBC), lambda b, s: (b, tok(s), 0)),
            pl.BlockSpec((1, HALO, SSD_XBC), lambda b, s: (b, jnp.maximum(tok(s) * hb - 1, 0), 0)),
            pl.BlockSpec((1, HALO, SSD_XBC), lambda b, s: (b, jnp.minimum((tok(s) + 1) * hb, t // HALO - 1), 0)),
            pl.BlockSpec((1, ts, LANES), lambda b, s: (b, tok_dt(s), phase(s))),
            pl.BlockSpec((1, ts, SSD_INNER), lambda b, s: (b, tok_out(s), 0)),
            _const_spec((8, SSD_XBC)),
            _const_spec((1, SSD_XBC)),
            pl.BlockSpec((1, SSD_HEADS, LANES), lambda b, s: (phase(s), 0, 0)),
            pl.BlockSpec((1, SSD_HEADS, LANES), lambda b, s: (phase(s), 0, 0)),
            _const_spec((1, SSD_INNER)),
            _const_spec((1, SSD_INNER)),
            pl.BlockSpec((1, 1, SSD_STATE, SSD_INNER), lambda b, s: (b, phase(s), 0, 0)),
        ],
        out_specs=[
            pl.BlockSpec((1, ts, SSD_INNER), lambda b, s: (b, tok_out(s), 0)),
            pl.BlockSpec((1, 1, SSD_STATE, SSD_INNER), lambda b, s: (b, phase(s), 0, 0)),
        ],
        scratch_shapes=[pltpu.VMEM((t, SSD_XBC), BF16), pltpu.VMEM((t, SSD_INNER), F32),
                        pltpu.VMEM((3, (ts // CHUNK) * SSD_HEADS, CHUNK), F32), pltpu.VMEM((ts, LANES), F32)],
        compiler_params=_params(2),
        name="ssd_scan",
    )(xbc, xbc, xbc, dt, z, conv_w_p, conv_b.reshape(1, SSD_XBC).astype(F32), a_log_p, dt_bias_p,
      d_full, norm_g.reshape(1, SSD_INNER).astype(F32), init)


def _cv_body(x_ref, w_ref, b_ref, lg_ref, lb_ref, o_ref, pad_ref, *, width, n_rows):
    zeros = jnp.zeros((CV_PAD, CONV_DIM), F32)
    pad_ref[0:CV_PAD, :] = zeros
    pad_ref[CV_PAD + width:, :] = zeros
    n_q = (CONV_WIDTH + 8) // 8

    def row_body(r, carry):
        xr = x_ref[r]
        u = xr[:, :CONV_DIM].astype(F32) * _sigmoid(xr[:, CONV_DIM:].astype(F32))
        pad_ref[CV_PAD:CV_PAD + width, :] = u
        cols = []
        for lt in range(CONV_DIM // LANES):
            ls = slice(lt * LANES, (lt + 1) * LANES)
            acc = jnp.broadcast_to(b_ref[:, ls], (width, LANES))
            for rr in range(8):
                shifted = pad_ref[rr:rr + width + 8 * (n_q - 1), ls]
                for q in range(n_q):
                    k = 8 * q + rr - (CV_PAD - CONV_WIDTH // 2)
                    if 0 <= k < CONV_WIDTH:
                        acc = acc + shifted[8 * q:8 * q + width] * w_ref[k:k + 1, ls]
            cols.append(acc)
        v = jnp.concatenate(cols, axis=1)
        mu = jnp.mean(v, axis=-1, keepdims=True)
        dv = v - mu
        var = jnp.mean(dv * dv, axis=-1, keepdims=True)
        yn = dv * lax.rsqrt(var + EPS) * lg_ref[...] + lb_ref[...]
        o_ref[r] = _silu(yn).astype(o_ref.dtype)
        return carry

    lax.fori_loop(0, n_rows, row_body, 0)


def _conv_branch(cv, width, dw_w, dw_b, ln_g, ln_b):
    n = cv.shape[0]
    rows = n // width
    n_rows = max(1, min(rows, 1024 // width))
    x3 = cv.reshape(rows, width, 2 * CONV_DIM)
    out = pl.pallas_call(
        functools.partial(_cv_body, width=width, n_rows=n_rows),
        out_shape=jax.ShapeDtypeStruct((rows, width, CONV_DIM), BF16),
        grid=(rows // n_rows,),
        in_specs=[pl.BlockSpec((n_rows, width, 2 * CONV_DIM), lambda i: (i, 0, 0)),
                  _const_spec((32, CONV_DIM)), _const_spec((1, CONV_DIM)),
                  _const_spec((1, CONV_DIM)), _const_spec((1, CONV_DIM))],
        out_specs=pl.BlockSpec((n_rows, width, CONV_DIM), lambda i: (i, 0, 0)),
        scratch_shapes=[pltpu.VMEM((width + 2 * CV_PAD, CONV_DIM), F32)],
        compiler_params=_params(1),
        name="conv_branch",
    )(x3, jnp.pad(dw_w.astype(F32), ((0, 32 - CONV_WIDTH), (0, 0))), dw_b.reshape(1, CONV_DIM).astype(F32),
      ln_g.reshape(1, CONV_DIM).astype(F32), ln_b.reshape(1, CONV_DIM).astype(F32))
    return out.reshape(n, CONV_DIM)


def _toeplitz_pairs(dw_w):
    n = GRID_W
    half = CONV_WIDTH // 2
    wf = dw_w[::-1].T
    q = jnp.pad(wf, ((0, 0), (n - 1 - half, n - half)))
    b = jnp.tile(q, (1, n))[:, :n * (2 * n - 1)].reshape(-1, n, 2 * n - 1)
    t = b[:, :, n - 1:]
    z = jnp.zeros_like(t)
    return jnp.concatenate([jnp.concatenate([t, z], axis=2), jnp.concatenate([z, t], axis=2)], axis=1).astype(BF16)


def _cvmx_body(a_ref, b_ref, t_ref, bias_ref, o_ref, *, n_ch):
    def ch_body(c, carry):
        u = a_ref[c].astype(F32) * _sigmoid(b_ref[c].astype(F32))
        o_ref[c] = (_dot(u.astype(BF16), t_ref[c]) + bias_ref[c]).astype(o_ref.dtype)
        return carry

    lax.fori_loop(0, n_ch, ch_body, 0, unroll=8)


def _conv_branch_mxu(cv_t, dw_w, dw_b):
    n = cv_t.shape[1]
    sr = n // LANES
    rb = min(256, sr)
    cbk = 64
    x3 = cv_t.reshape(2 * CONV_DIM, sr, LANES)
    toe = _toeplitz_pairs(dw_w)
    bias = jnp.broadcast_to(dw_b.astype(F32)[:, None, None], (CONV_DIM, 1, LANES))
    nblk = CONV_DIM // cbk
    out = pl.pallas_call(
        functools.partial(_cvmx_body, n_ch=cbk),
        out_shape=jax.ShapeDtypeStruct((CONV_DIM, sr, LANES), BF16),
        grid=(nblk, sr // rb),
        in_specs=[pl.BlockSpec((cbk, rb, LANES), lambda c, i: (c, i, 0)),
                  pl.BlockSpec((cbk, rb, LANES), lambda c, i: (c + nblk, i, 0)),
                  pl.BlockSpec((cbk, LANES, LANES), lambda c, i: (c, 0, 0)),
                  pl.BlockSpec((cbk, 1, LANES), lambda c, i: (c, 0, 0))],
        out_specs=pl.BlockSpec((cbk, rb, LANES), lambda c, i: (c, i, 0)),
        compiler_params=_params(2),
        name="conv_branch_mxu",
    )(x3, x3, toe, bias)
    return out.reshape(CONV_DIM, n)


def _pool_body(x_ref, ic_ref, o_ref, *, n):
    max_lo = max(POOL_WINDOWS) // 2
    for k, w in enumerate(POOL_WINDOWS):
        ls = slice(k * POOL_PAD, (k + 1) * POOL_PAD)
        x = x_ref[0, :, :, ls].astype(F32)
        zeros = jnp.zeros((max_lo,) + x.shape[1:], F32)
        run = jnp.concatenate([zeros, x, zeros], axis=0)
        span = 1
        while span < w:
            run = run[:-span] + run[span:]
            span *= 2
        lo = w // 2
        total = run[max_lo - lo:max_lo - lo + n]
        o_ref[0, :, :, ls] = (total * ic_ref[:, :, ls] - x).astype(o_ref.dtype)


def _pool_branch(pool_in, n, cb):
    nb, _, c, _ = pool_in.shape
    pos = np.arange(n)
    inv = np.zeros((n, 1, POOL_PADDED), np.float32)
    for k, w in enumerate(POOL_WINDOWS):
        lo = w // 2
        cnt = np.clip(pos + (w - 1 - lo) + 1, 0, n) - np.clip(pos - lo, 0, n)
        inv[:, 0, k * POOL_PAD:(k + 1) * POOL_PAD] = (1.0 / cnt)[:, None]
    return pl.pallas_call(
        functools.partial(_pool_body, n=n),
        out_shape=jax.ShapeDtypeStruct(pool_in.shape, BF16),
        grid=(nb, c // cb),
        in_specs=[pl.BlockSpec((1, n, cb, POOL_PADDED), lambda b, j: (b, 0, j, 0)),
                  _const_spec((n, 1, POOL_PADDED))],
        out_specs=pl.BlockSpec((1, n, cb, POOL_PADDED), lambda b, j: (b, 0, j, 0)),
        compiler_params=_params(2),
        name="pool_branch",
    )(pool_in, jnp.asarray(inv))


def _merge_body(x_ref, ys_ref, uc_ref, pp_ref, gt_ref, mg_ref, ps_ref, lg_ref, lb_ref, wss_ref, wcv_ref, wpl_ref,
                wo_ref, o_ref, *, cv_transposed):
    d = D_MODEL
    y_ssd = _dot(ys_ref[...], wss_ref[...])
    if cv_transposed:
        v = uc_ref[...].astype(F32)
        mu = jnp.mean(v, axis=0, keepdims=True)
        dv = v - mu
        var = jnp.mean(dv * dv, axis=0, keepdims=True)
        reps = v.shape[1] // LANES
        yn = dv * lax.rsqrt(var + EPS) * jnp.tile(lg_ref[...], (1, reps)) + jnp.tile(lb_ref[...], (1, reps))
        y_cv = lax.dot_general(_silu(yn).astype(BF16), wcv_ref[...], (((0,), (0,)), ((), ())),
                               preferred_element_type=F32)
    else:
        y_cv = _dot(uc_ref[...], wcv_ref[...])
    y_pool = jnp.concatenate(
        [_dot(pp_ref[:, k * POOL_PAD:(k + 1) * POOL_PAD], wpl_ref[k]) for k in range(POOL_GROUPS)], axis=1)
    y_pool = y_pool * ps_ref[...]
    merged = (_sigmoid(gt_ref[:, 0:d].astype(F32)) * y_ssd
              + _sigmoid(gt_ref[:, d:2 * d].astype(F32)) * y_cv
              + _sigmoid(gt_ref[:, 2 * d:3 * d].astype(F32)) * y_pool)
    o_ref[...] = x_ref[...] + mg_ref[0] * _dot(merged.astype(BF16), wo_ref[...])


def _merge(x, ys, uc, pp, gates, mod_gate, pool_scale, ln_g, ln_b, w_ssd, w_cv, w_pool, w_out, tm, tiles_per_mod,
           cv_transposed):
    n, d = x.shape
    tok = lambda w: pl.BlockSpec((tm, w), lambda i: (i, 0))
    uc_spec = pl.BlockSpec((CONV_DIM, tm), lambda i: (0, i)) if cv_transposed else tok(CONV_DIM)
    ln_rows = lambda v: jnp.broadcast_to(v.astype(F32)[:, None], (CONV_DIM, LANES))
    return pl.pallas_call(
        functools.partial(_merge_body, cv_transposed=cv_transposed),
        out_shape=jax.ShapeDtypeStruct((n, d), F32),
        grid=(n // tm,),
        in_specs=[tok(d), tok(SSD_INNER), uc_spec, tok(POOL_PADDED), tok(3 * d),
                  pl.BlockSpec((1, 1, d), lambda i: (i // tiles_per_mod, 0, 0)),
                  _const_spec((1, d)), _const_spec((CONV_DIM, LANES)), _const_spec((CONV_DIM, LANES)),
                  _const_spec(w_ssd.shape), _const_spec(w_cv.shape),
                  _const_spec(w_pool.shape), _const_spec(w_out.shape)],
        out_specs=tok(d),
        compiler_params=_params(1),
        name="merge",
    )(x, ys, uc, pp, gates, mod_gate, pool_scale.reshape(1, d).astype(F32), ln_rows(ln_g), ln_rows(ln_b),
      w_ssd, w_cv, w_pool, w_out)


def _ffn_body(x_ref, g_ref, sh_ref, sc_ref, gt_ref, wg_ref, wu_ref, wd_ref, fg_ref, o_ref, *, final, hid_chunk):
    x = x_ref[...]
    hb = _norm_mod(x, g_ref[...], sh_ref[0], sc_ref[0]).astype(BF16)
    acc = jnp.zeros(x.shape, F32)
    for j in range(0, FFN_HIDDEN, hid_chunk):
        gate = _dot(hb, wg_ref[:, j:j + hid_chunk])
        up = _dot(hb, wu_ref[:, j:j + hid_chunk])
        acc = acc + _dot((_silu(gate) * up).astype(BF16), wd_ref[j:j + hid_chunk, :])
    out = x + gt_ref[0] * acc
    if final:
        ms = jnp.mean(out * out, axis=-1, keepdims=True)
        out = (out * lax.rsqrt(ms + EPS)) * fg_ref[...]
    o_ref[...] = out


def _ffn(x, g, shift, scale, gate, w_gate, w_up, w_down, final_g, final, tm, tiles_per_mod):
    n, d = x.shape
    mod_spec = pl.BlockSpec((1, 1, d), lambda i: (i // tiles_per_mod, 0, 0))
    return pl.pallas_call(
        functools.partial(_ffn_body, final=final, hid_chunk=FFN_HIDDEN // 2),
        out_shape=jax.ShapeDtypeStruct((n, d), F32),
        grid=(n // tm,),
        in_specs=[pl.BlockSpec((tm, d), lambda i: (i, 0)), _const_spec((1, d)), mod_spec, mod_spec, mod_spec,
                  _const_spec(w_gate.shape), _const_spec(w_up.shape), _const_spec(w_down.shape),
                  _const_spec((1, d))],
        out_specs=pl.BlockSpec((tm, d), lambda i: (i, 0)),
        compiler_params=_params(1),
        name="ffn",
    )(x, g.reshape(1, d).astype(F32), shift, scale, gate, w_gate, w_up, w_down,
      final_g.reshape(1, d).astype(F32))


def _split_w_in(w):
    o = np.cumsum([0, SSD_INNER, SSD_XBC, 2 * SSD_HEADS, 2 * CONV_DIM, POOL_DIM, 3 * D_MODEL])
    seg = [w[:, o[i]:o[i + 1]] for i in range(6)]
    z, xbc, dt, cv, pool, gates = seg
    dt = jnp.concatenate([jnp.pad(dt[:, :SSD_HEADS], ((0, 0), (0, LANES - SSD_HEADS))),
                          jnp.pad(dt[:, SSD_HEADS:], ((0, 0), (0, LANES - SSD_HEADS)))], axis=1)
    pool = jnp.pad(pool.reshape(-1, POOL_GROUPS, POOL_GROUP_DIM),
                   ((0, 0), (0, 0), (0, POOL_PAD - POOL_GROUP_DIM))).reshape(-1, POOL_PADDED)
    return [v.astype(BF16) for v in (z, xbc, dt, cv, pool, gates)]


def kernel(x, c, ctx, c_ctx, w_ada, b_ada, norm1_g, norm2_g, w_in, ssd_conv_w, ssd_conv_b, ssd_a_log,
           ssd_dt_bias, ssd_d, ssd_norm_g, ssd_w_out, cv_dw_w, cv_dw_b, cv_ln_g, cv_ln_b, cv_w_out, pool_w,
           pool_scale, w_out, ffn_w_gate, ffn_w_up, ffn_w_down, final_g):
    nb, t, d = x.shape
    tc = ctx.shape[1]
    depth = w_in.shape[0]
    rows = t // GRID_W

    n_cond = -(-(nb + 1) // 8) * 8
    cc = jnp.concatenate([c, c_ctx[None, :], jnp.zeros((n_cond - nb - 1, d), F32)], axis=0)
    mods = _ada(cc, w_ada, b_ada)

    tm_lat = min(512, t)
    tm_ctx = min(256, tc)
    ts_lat = min(512, t)
    ts_ctx = min(256, tc)
    cb_lat = min(16, GRID_W)

    xl = x.reshape(nb * t, d)
    xc = ctx.reshape(nb * tc, d)
    zero_state = jnp.zeros((nb, 2, SSD_STATE, SSD_INNER), F32)

    for l in range(depth):
        last = l == depth - 1
        w_z, w_xbc, w_dt, w_cv, w_pool, w_gates = _split_w_in(w_in[l])
        w_ssd_o = ssd_w_out[l].astype(BF16)
        w_cv_o = cv_w_out[l].astype(BF16)
        w_pool_o = jnp.pad(pool_w[l], ((0, 0), (0, POOL_PAD - POOL_GROUP_DIM), (0, 0))).astype(BF16)
        w_o = w_out[l].astype(BF16)
        w_fg, w_fu, w_fd = (v[l].astype(BF16) for v in (ffn_w_gate, ffn_w_up, ffn_w_down))
        mod_lat = [mods[l, :nb, k * d:(k + 1) * d].reshape(nb, 1, d) for k in range(N_MOD)]
        mod_ctx = [mods[l, nb:nb + 1, k * d:(k + 1) * d].reshape(1, 1, d) for k in range(N_MOD)]
        ssd_args = (ssd_conv_w[l], ssd_conv_b[l], ssd_a_log[l], ssd_dt_bias[l], ssd_d[l], ssd_norm_g[l])
        cv_args = (cv_dw_w[l], cv_dw_b[l], cv_ln_g[l], cv_ln_b[l])

        def mixer_tail(xs, z_y, uc, cv_transposed, pool4, gates, pool_n, pool_cb, mod, tm, tpm):
            pp = _pool_branch(pool4, pool_n, pool_cb).reshape(-1, POOL_PADDED)
            return _merge(xs, z_y, uc, pp, gates, mod[2], pool_scale[l], cv_ln_g[l], cv_ln_b[l],
                          w_ssd_o, w_cv_o, w_pool_o, w_o, tm, tpm, cv_transposed)

        n_ctx_tiles = nb * tc // tm_ctx
        if last:
            xbc_c, dt_c = _inproj(xc, norm1_g[l], mod_ctx[0], mod_ctx[1], [w_xbc, w_dt], [BF16, F32],
                                  tm_ctx, n_ctx_tiles)
            z_c = jnp.zeros((nb, tc, SSD_INNER), BF16)
        else:
            z_c, xbc_c, dt_c, cv_c, pool_c, gates_c = _inproj(
                xc, norm1_g[l], mod_ctx[0], mod_ctx[1], [w_z, w_xbc, w_dt, w_cv, w_pool, w_gates],
                [BF16, BF16, F32, BF16, F32, BF16], tm_ctx, n_ctx_tiles)
            z_c = z_c.reshape(nb, tc, SSD_INNER)
        y_c, ctx_state = _ssd(xbc_c.reshape(nb, tc, SSD_XBC), dt_c.reshape(nb, tc, DT_PAD), z_c,
                              *ssd_args, zero_state, ts_ctx)
        if not last:
            uc_c = _conv_branch(cv_c, tc, *cv_args)
            xc = mixer_tail(xc, y_c.reshape(nb * tc, SSD_INNER), uc_c, False, pool_c.reshape(nb, tc, 1, POOL_PADDED),
                            gates_c, tc, 1, mod_ctx, tm_ctx, n_ctx_tiles)
            xc = _ffn(xc, norm2_g[l], mod_ctx[3], mod_ctx[4], mod_ctx[5], w_fg, w_fu, w_fd, final_g, False,
                      tm_ctx, n_ctx_tiles)

        tpm = t // tm_lat
        z_l, xbc_l, dt_l, cv_l, pool_l, gates_l = _inproj(
            xl, norm1_g[l], mod_lat[0], mod_lat[1], [w_z, w_xbc, w_dt, w_cv.T, w_pool, w_gates],
            [BF16, BF16, F32, BF16, BF16, BF16], tm_lat, tpm, (False, False, False, True, False, False))
        y_l, _ = _ssd(xbc_l.reshape(nb, t, SSD_XBC), dt_l.reshape(nb, t, DT_PAD), z_l.reshape(nb, t, SSD_INNER),
                      *ssd_args, ctx_state, ts_lat)
        v_l = _conv_branch_mxu(cv_l, cv_dw_w[l], cv_dw_b[l])
        xl = mixer_tail(xl, y_l.reshape(nb * t, SSD_INNER), v_l, True, pool_l.reshape(nb, rows, GRID_W, POOL_PADDED),
                        gates_l, rows, cb_lat, mod_lat, tm_lat, tpm)
        xl = _ffn(xl, norm2_g[l], mod_lat[3], mod_lat[4], mod_lat[5], w_fg, w_fu, w_fd, final_g, last,
                  tm_lat, tpm)

    return xl.reshape(nb, t, d)
```

```python
import functools

import numpy as np
import jax
import jax.numpy as jnp
from jax import lax
from jax.experimental import pallas as pl
from jax.experimental.pallas import tpu as pltpu

F32 = jnp.float32
BF16 = jnp.bfloat16

D_MODEL = 1024
GRID_W = 64
EPS = 1e-6
N_MOD = 6
SSD_HEADS = 16
SSD_HEAD_DIM = 64
SSD_INNER = SSD_HEADS * SSD_HEAD_DIM
SSD_GROUPS = 2
SSD_STATE = 128
SSD_CONV = 5
SSD_XBC = SSD_INNER + 2 * SSD_GROUPS * SSD_STATE
CONV_DIM = 768
CONV_WIDTH = 31
POOL_DIM = 768
POOL_WINDOWS = (2, 4, 8, 16)
POOL_GROUPS = len(POOL_WINDOWS)
POOL_GROUP_DIM = POOL_DIM // POOL_GROUPS
POOL_OUT_DIM = D_MODEL // POOL_GROUPS
FFN_HIDDEN = 2816

LANES = 128
MXU_DIM = 256
CHUNK = 128
LOG2E = 1.4426950408889634
POOL_PAD = 256
POOL_PADDED = POOL_GROUPS * POOL_PAD
DT_PAD = 2 * LANES
HALO = 16
CV_PAD = 16
VMEM_LIMIT = 56 * 1024 * 1024


def _layer_spec(arr, l):
    shape = arr.shape[1:]
    nd = len(shape)
    return pl.BlockSpec((1,) + shape, lambda *_: (l,) + (0,) * nd, pipeline_mode=pl.Buffered(1))


def _mod_spec(l, k, row_of_tile):
    return pl.BlockSpec((1, 1, 1, D_MODEL), lambda i: (l, row_of_tile(i), 0, k))


def _params(n_axes):
    return pltpu.CompilerParams(dimension_semantics=("arbitrary",) * n_axes, vmem_limit_bytes=VMEM_LIMIT)


def _sigmoid(v):
    return 0.5 * jnp.tanh(0.5 * v) + 0.5


def _silu(v):
    return v * _sigmoid(v)


def _dot(a, b):
    return jnp.dot(a, b, preferred_element_type=F32)


def _dot_nt(a, b):
    return lax.dot_general(a, b, (((1,), (1,)), ((), ())), preferred_element_type=F32)


def _ada_body(c_ref, w_ref, b_ref, o_ref):
    s = _silu(c_ref[...])
    o_ref[0] = jnp.dot(s, w_ref[0], preferred_element_type=F32,
                       precision=lax.Precision.HIGHEST) + b_ref[0]


def _ada(cc, w_ada, b_ada):
    nl, d, nm = w_ada.shape
    r = cc.shape[0]
    tn = 1024
    return pl.pallas_call(
        _ada_body,
        out_shape=jax.ShapeDtypeStruct((nl, r, nm), F32),
        grid=(nl, nm // tn),
        in_specs=[pl.BlockSpec((r, d), lambda l, j: (0, 0)),
                  pl.BlockSpec((1, d, tn), lambda l, j: (l, 0, j)),
                  pl.BlockSpec((1, 1, tn), lambda l, j: (l, 0, j))],
        out_specs=pl.BlockSpec((1, r, tn), lambda l, j: (l, 0, j)),
        compiler_params=_params(2),
        name="adaln",
    )(cc, w_ada, b_ada.reshape(nl, 1, nm)).reshape(nl, r, 1, nm)


def _norm_mod(x, g, shift, scale):
    ms = jnp.mean(x * x, axis=-1, keepdims=True)
    h = (x * lax.rsqrt(ms + EPS)) * g
    return h * (1.0 + scale) + shift


def _inproj_body(*refs, n_seg, col_chunk, transposed, conv_seg, tiles_per_seq):
    x_ref, xp_ref, xn_ref, g_ref, sh_ref, sc_ref, cw_ref, cb_ref = refs[:8]
    w_refs = refs[8:8 + n_seg]
    o_refs = refs[8 + n_seg:]
    tm = x_ref.shape[0]
    norm = lambda v: _norm_mod(v, g_ref[0], sh_ref[0, 0], sc_ref[0, 0]).astype(BF16)
    hb = norm(x_ref[...])
    pos = pl.program_id(0) % tiles_per_seq
    h_prev = jnp.where(pos > 0, norm(xp_ref[...]), jnp.zeros((HALO, x_ref.shape[1]), BF16))
    h_next = jnp.where(pos < tiles_per_seq - 1, norm(xn_ref[...]), jnp.zeros((HALO, x_ref.shape[1]), BF16))
    hb_ext = jnp.concatenate([h_prev, hb, h_next], axis=0)
    for seg, (w_ref, o_ref, tr) in enumerate(zip(w_refs, o_refs, transposed)):
        n = w_ref.shape[1] if tr else w_ref.shape[2]
        step = min(col_chunk, n)
        for j in range(0, n, step):
            if tr:
                o_ref[j:j + step, :] = _dot_nt(w_ref[0, j:j + step, :], hb).astype(o_ref.dtype)
            elif seg == conv_seg:
                r = _dot(hb_ext, w_ref[0, :, j:j + step])
                rows = tm + 2 * HALO
                acc = jnp.broadcast_to(cb_ref[0, :, j:j + step], (tm, step))
                for k in range(SSD_CONV):
                    sh = (SSD_CONV // 2 - k) % rows
                    rk = r if sh == 0 else pltpu.roll(r, sh, axis=0)
                    acc = acc + rk[HALO:HALO + tm] * cw_ref[0, k:k + 1, j:j + step]
                hv = 0.5 * acc
                o_ref[:, j:j + step] = (hv + hv * jnp.tanh(hv)).astype(o_ref.dtype)
            else:
                o_ref[:, j:j + step] = _dot(hb, w_ref[0, :, j:j + step]).astype(o_ref.dtype)


def _inproj(x, l, norm_g, mods, row_of_tile, weights, out_dtypes, tm, seq_len, conv_seg, conv_w, conv_b,
            transposed=None):
    n, d = x.shape
    n_seg = len(weights)
    transposed = tuple(transposed or (False,) * n_seg)
    out_shape, out_specs = [], []
    for w, dt, tr in zip(weights, out_dtypes, transposed):
        if tr:
            out_shape.append(jax.ShapeDtypeStruct((w.shape[1], n), dt))
            out_specs.append(pl.BlockSpec((w.shape[1], tm), lambda i: (0, i)))
        else:
            out_shape.append(jax.ShapeDtypeStruct((n, w.shape[2]), dt))
            out_specs.append(pl.BlockSpec((tm, w.shape[2]), lambda i: (i, 0)))
    hb = tm // HALO
    return pl.pallas_call(
        functools.partial(_inproj_body, n_seg=n_seg, col_chunk=512, transposed=transposed, conv_seg=conv_seg,
                          tiles_per_seq=seq_len // tm),
        out_shape=out_shape,
        grid=(n // tm,),
        in_specs=[pl.BlockSpec((tm, d), lambda i: (i, 0)),
                  pl.BlockSpec((HALO, d), lambda i: (jnp.maximum(i * hb - 1, 0), 0)),
                  pl.BlockSpec((HALO, d), lambda i: (jnp.minimum((i + 1) * hb, n // HALO - 1), 0)),
                  _layer_spec(norm_g, l), _mod_spec(l, 0, row_of_tile), _mod_spec(l, 1, row_of_tile),
                  _layer_spec(conv_w, l), _layer_spec(conv_b, l)]
                 + [_layer_spec(w, l) for w in weights],
        out_specs=out_specs,
        compiler_params=_params(1),
        name="inproj",
    )(x, x, x, norm_g, mods, mods, conv_w, conv_b, *weights)


def _split3(v):
    hi = v.astype(BF16)
    r1 = v - hi.astype(F32)
    mid = r1.astype(BF16)
    lo = (r1 - mid.astype(F32)).astype(BF16)
    return hi, mid, lo


def _ssd_body(xm_ref, dt_ref, z_ref, alog_ref, dtb_ref, dsk_ref,
              ng_ref, init_ref, y_ref, st_ref, yf_all, pt_ref, ec_ref, *, n_steps, ts):
    s = pl.program_id(1)
    fwd = s < n_steps
    c = jnp.where(fwd, s, 2 * n_steps - 1 - s)
    base = pl.multiple_of(c * ts, ts)
    n_sub = ts // CHUNK
    nh = SSD_HEADS
    inner = SSD_INNER
    half_grp = SSD_GROUPS * SSD_STATE
    pairs_per_group = SSD_HEADS // SSD_GROUPS // 2

    @pl.when(jnp.logical_or(s == 0, s == n_steps))
    def _():
        st_ref[...] = init_ref[...]

    @pl.when(fwd)
    def _():
        yf_all[pl.ds(base, ts), :] = xm_ref[0, :, :inner].astype(F32) * dsk_ref[0]

    li = lax.broadcasted_iota(jnp.int32, (CHUNK, CHUNK), 0)
    si = lax.broadcasted_iota(jnp.int32, (CHUNK, CHUNK), 1)
    tri = jnp.where(fwd, li - si, si - li) >= 0
    first_half = si < SSD_HEAD_DIM
    first_half_row = first_half[0:1]

    dtr = dt_ref[0]
    dtr_t = jnp.concatenate([dtr[j * CHUNK:(j + 1) * CHUNK].T[:nh] for j in range(n_sub)], axis=0)
    dtr_t = dtr_t + jnp.tile(dtb_ref[0, 0], (n_sub, 1))
    dt_t = jnp.maximum(dtr_t, 0.0) + jnp.log(1.0 + jnp.exp(-jnp.abs(dtr_t)))
    a2_t = dt_t * jnp.tile(-jnp.exp(alog_ref[0, 0]) * LOG2E, (n_sub, 1))
    order = jnp.where(tri, 1.0, 0.0).astype(BF16)
    pieces = _split3(a2_t)
    e2_t = sum(_dot_nt(p, order) for p in pieces)
    tot2 = jnp.sum(a2_t, axis=1, keepdims=True)
    er2_t = e2_t - jnp.log(dt_t) * LOG2E
    pt_ref[0] = er2_t
    pt_ref[1] = jnp.exp2(tot2 - er2_t)
    pt_ref[2] = jnp.broadcast_to(jnp.exp2(tot2), e2_t.shape)
    zpad = jnp.zeros((CHUNK - nh, CHUNK), BF16)
    for j in range(n_sub):
        ec_ref[j * CHUNK:(j + 1) * CHUNK, :] = sum(
            _dot_nt(order, jnp.concatenate([p[j * nh:(j + 1) * nh], zpad], axis=0)) for p in pieces)

    for j in range(n_sub):
        jj = jnp.where(fwd, j, n_sub - 1 - j)
        loc = pl.multiple_of(jj * CHUNK, CHUNK)
        off = pl.multiple_of(base + loc, CHUNK)
        hrow = pl.multiple_of(jj * nh, nh)
        xc = xm_ref[0, pl.ds(loc, CHUNK), :]
        er2 = pt_ref[0, pl.ds(hrow, nh), :]
        w_t = pt_ref[1, pl.ds(hrow, nh), :]
        etot = pt_ref[2, pl.ds(hrow, nh), :]
        e_col = ec_ref[pl.ds(loc, CHUNK), :]
        y_pairs = []
        for g in range(SSD_GROUPS):
            b_g = xc[:, inner + g * SSD_STATE: inner + (g + 1) * SSD_STATE]
            c_g = xc[:, inner + half_grp + g * SSD_STATE: inner + half_grp + (g + 1) * SSD_STATE]
            cb = _dot_nt(c_g, b_g).astype(BF16)
            b_t = b_g.astype(F32).T.astype(BF16)
            gcols = slice(g * inner // SSD_GROUPS, (g + 1) * inner // SSD_GROUPS)
            y_carry = _dot(c_g, st_ref[0, 0, :, gcols].astype(BF16))
            for pr in range(pairs_per_group):
                pair = g * pairs_per_group + pr
                cols = slice(pair * LANES, (pair + 1) * LANES)
                xp = xc[:, cols]
                zero = jnp.zeros_like(xp)
                x12 = jnp.concatenate([jnp.where(first_half, xp, zero), jnp.where(first_half, zero, xp)], axis=0)
                ms, bw, dec = [], [], []
                for h in (2 * pair, 2 * pair + 1):
                    ecol = jnp.broadcast_to(e_col[:, h:h + 1], (CHUNK, CHUNK))
                    ms.append(jnp.exp2(jnp.where(tri, ecol - er2[h:h + 1, :], -1e30)).astype(BF16) * cb)
                    dec.append(jnp.exp2(ecol))
                    w_rows = jnp.broadcast_to(w_t[h:h + 1, :], (2 * 8, CHUNK)).astype(BF16)
                    bw.append(b_t * jnp.tile(w_rows, (SSD_STATE // (2 * 8), 1)))
                both = _dot(jnp.concatenate([jnp.concatenate(ms, axis=1), jnp.concatenate(bw, axis=1)], axis=0), x12)
                y_pairs.append(both[:CHUNK] + jnp.where(first_half, dec[0], dec[1]) * y_carry[:, pr * LANES:(pr + 1) * LANES])
                erow = jnp.where(first_half_row, etot[2 * pair:2 * pair + 1, :], etot[2 * pair + 1:2 * pair + 2, :])
                st_ref[0, 0, :, cols] = st_ref[0, 0, :, cols] * erow + both[CHUNK:]
        yf_all[pl.ds(off, CHUNK), :] = yf_all[pl.ds(off, CHUNK), :] + jnp.concatenate(y_pairs, axis=1)

    @pl.when(jnp.logical_not(fwd))
    def _():
        y = yf_all[pl.ds(base, ts), :] * _silu(z_ref[0].astype(F32))
        gw = inner // SSD_GROUPS
        for g in range(SSD_GROUPS):
            yg = y[:, g * gw:(g + 1) * gw]
            ms = jnp.mean(yg * yg, axis=-1, keepdims=True)
            yn = (yg * lax.rsqrt(ms + EPS)) * ng_ref[0, :, g * gw:(g + 1) * gw]
            y_ref[0, :, g * gw:(g + 1) * gw] = yn.astype(y_ref.dtype)


def _ssd(xbc, dt, z, l, prm, init, ts):
    nb, t, _ = xbc.shape
    n_steps = t // ts

    def tok_out(s):
        return jnp.where(s < n_steps, n_steps - 1, 2 * n_steps - 1 - s)

    def tok_dt(s):
        return jnp.where(s < n_steps, s, 2 * n_steps - 1 - s)

    def phase(s):
        return jnp.where(s < n_steps, 0, 1)

    head_spec = pl.BlockSpec((1, 1, SSD_HEADS, LANES), lambda b, s: (l, phase(s), 0, 0))
    return pl.pallas_call(
        functools.partial(_ssd_body, n_steps=n_steps, ts=ts),
        out_shape=[jax.ShapeDtypeStruct((nb, t, SSD_INNER), BF16),
                   jax.ShapeDtypeStruct((nb, 2, SSD_STATE, SSD_INNER), F32)],
        grid=(nb, 2 * n_steps),
        in_specs=[
            pl.BlockSpec((1, ts, SSD_XBC), lambda b, s: (b, tok_dt(s), 0)),
            pl.BlockSpec((1, ts, LANES), lambda b, s: (b, tok_dt(s), phase(s))),
            pl.BlockSpec((1, ts, SSD_INNER), lambda b, s: (b, tok_out(s), 0)),
            head_spec,
            head_spec,
            _layer_spec(prm["d_full"], l),
            _layer_spec(prm["norm_g"], l),
            pl.BlockSpec((1, 1, SSD_STATE, SSD_INNER), lambda b, s: (b, phase(s), 0, 0)),
        ],
        out_specs=[
            pl.BlockSpec((1, ts, SSD_INNER), lambda b, s: (b, tok_out(s), 0)),
            pl.BlockSpec((1, 1, SSD_STATE, SSD_INNER), lambda b, s: (b, phase(s), 0, 0)),
        ],
        scratch_shapes=[pltpu.VMEM((t, SSD_INNER), F32),
                        pltpu.VMEM((3, (ts // CHUNK) * SSD_HEADS, CHUNK), F32), pltpu.VMEM((ts, LANES), F32)],
        compiler_params=_params(2),
        name="ssd_scan",
    )(xbc, dt, z, prm["a_log"], prm["dt_bias"], prm["d_full"], prm["norm_g"], init)


def _ssd_params(a_log, dt_bias, d_skip, norm_g):
    nl = a_log.shape[0]
    rep = lambda v: jnp.broadcast_to(v[:, :, :, None], (nl, 2, SSD_HEADS, LANES))
    return dict(
        a_log=rep(a_log), dt_bias=rep(dt_bias),
        d_full=jnp.repeat(d_skip, SSD_HEAD_DIM, axis=1).reshape(nl, 1, SSD_INNER),
        norm_g=norm_g.reshape(nl, 1, SSD_INNER))


def _cv_body(x_ref, w_ref, b_ref, lg_ref, lb_ref, o_ref, pad_ref, *, width, n_rows):
    zeros = jnp.zeros((CV_PAD, CONV_DIM), F32)
    pad_ref[0:CV_PAD, :] = zeros
    pad_ref[CV_PAD + width:, :] = zeros
    n_q = (CONV_WIDTH + 8) // 8

    def row_body(r, carry):
        xr = x_ref[r]
        u = xr[:, :CONV_DIM].astype(F32) * _sigmoid(xr[:, CONV_DIM:].astype(F32))
        pad_ref[CV_PAD:CV_PAD + width, :] = u
        cols = []
        for lt in range(CONV_DIM // LANES):
            ls = slice(lt * LANES, (lt + 1) * LANES)
            acc = jnp.broadcast_to(b_ref[0, :, ls], (width, LANES))
            for rr in range(8):
                shifted = pad_ref[rr:rr + width + 8 * (n_q - 1), ls]
                for q in range(n_q):
                    k = 8 * q + rr - (CV_PAD - CONV_WIDTH // 2)
                    if 0 <= k < CONV_WIDTH:
                        acc = acc + shifted[8 * q:8 * q + width] * w_ref[0, k:k + 1, ls]
            cols.append(acc)
        v = jnp.concatenate(cols, axis=1)
        mu = jnp.mean(v, axis=-1, keepdims=True)
        dv = v - mu
        var = jnp.mean(dv * dv, axis=-1, keepdims=True)
        yn = dv * lax.rsqrt(var + EPS) * lg_ref[0] + lb_ref[0]
        o_ref[r] = _silu(yn).astype(o_ref.dtype)
        return carry

    lax.fori_loop(0, n_rows, row_body, 0)


def _conv_branch(cv, width, l, dw_w, dw_b, ln_g, ln_b):
    n = cv.shape[0]
    rows = n // width
    n_rows = max(1, min(rows, 1024 // width))
    x3 = cv.reshape(rows, width, 2 * CONV_DIM)
    out = pl.pallas_call(
        functools.partial(_cv_body, width=width, n_rows=n_rows),
        out_shape=jax.ShapeDtypeStruct((rows, width, CONV_DIM), BF16),
        grid=(rows // n_rows,),
        in_specs=[pl.BlockSpec((n_rows, width, 2 * CONV_DIM), lambda i: (i, 0, 0)),
                  _layer_spec(dw_w, l), _layer_spec(dw_b, l), _layer_spec(ln_g, l), _layer_spec(ln_b, l)],
        out_specs=pl.BlockSpec((n_rows, width, CONV_DIM), lambda i: (i, 0, 0)),
        scratch_shapes=[pltpu.VMEM((width + 2 * CV_PAD, CONV_DIM), F32)],
        compiler_params=_params(1),
        name="conv_branch",
    )(x3, dw_w, dw_b, ln_g, ln_b)
    return out.reshape(n, CONV_DIM)


def _toeplitz_pairs(dw_w):
    n = GRID_W
    half = CONV_WIDTH // 2
    wf = jnp.swapaxes(dw_w[:, ::-1, :], 1, 2).reshape(-1, CONV_WIDTH)
    q = jnp.pad(wf, ((0, 0), (n - 1 - half, n - half)))
    b = jnp.tile(q, (1, n))[:, :n * (2 * n - 1)].reshape(-1, n, 2 * n - 1)
    t = b[:, :, n - 1:]
    z = jnp.zeros_like(t)
    return jnp.concatenate([jnp.concatenate([t, z], axis=2), jnp.concatenate([z, t], axis=2)], axis=1).astype(BF16)


def _cvmx_body(a_ref, b_ref, t_ref, bias_ref, o_ref, *, n_ch):
    def ch_body(c, carry):
        u = a_ref[c].astype(F32) * _sigmoid(b_ref[c].astype(F32))
        o_ref[c] = (_dot(u.astype(BF16), t_ref[c]) + bias_ref[c]).astype(o_ref.dtype)
        return carry

    lax.fori_loop(0, n_ch, ch_body, 0, unroll=8)


def _conv_branch_mxu(cv_t, l, toe, bias):
    n = cv_t.shape[1]
    sr = n // LANES
    rb = min(256, sr)
    cbk = 64
    x3 = cv_t.reshape(2 * CONV_DIM, sr, LANES)
    nblk = CONV_DIM // cbk
    out = pl.pallas_call(
        functools.partial(_cvmx_body, n_ch=cbk),
        out_shape=jax.ShapeDtypeStruct((CONV_DIM, sr, LANES), BF16),
        grid=(nblk, sr // rb),
        in_specs=[pl.BlockSpec((cbk, rb, LANES), lambda c, i: (c, i, 0)),
                  pl.BlockSpec((cbk, rb, LANES), lambda c, i: (c + nblk, i, 0)),
                  pl.BlockSpec((cbk, LANES, LANES), lambda c, i: (l * nblk + c, 0, 0)),
                  pl.BlockSpec((cbk, 1, LANES), lambda c, i: (l * nblk + c, 0, 0))],
        out_specs=pl.BlockSpec((cbk, rb, LANES), lambda c, i: (c, i, 0)),
        compiler_params=_params(2),
        name="conv_branch_mxu",
    )(x3, x3, toe, bias)
    return out.reshape(CONV_DIM, n)


def _pool_body(x_ref, ic_ref, o_ref, *, n):
    max_lo = max(POOL_WINDOWS) // 2
    for k, w in enumerate(POOL_WINDOWS):
        ls = slice(k * POOL_PAD, (k + 1) * POOL_PAD)
        x = x_ref[0, :, :, ls].astype(F32)
        zeros = jnp.zeros((max_lo,) + x.shape[1:], F32)
        run = jnp.concatenate([zeros, x, zeros], axis=0)
        span = 1
        while span < w:
            run = run[:-span] + run[span:]
            span *= 2
        lo = w // 2
        total = run[max_lo - lo:max_lo - lo + n]
        o_ref[0, :, :, ls] = (total * ic_ref[:, :, ls] - x).astype(o_ref.dtype)


def _pool_branch(pool_in, n, cb):
    nb, _, c, _ = pool_in.shape
    pos = np.arange(n)
    inv = np.zeros((n, 1, POOL_PADDED), np.float32)
    for k, w in enumerate(POOL_WINDOWS):
        lo = w // 2
        cnt = np.clip(pos + (w - 1 - lo) + 1, 0, n) - np.clip(pos - lo, 0, n)
        inv[:, 0, k * POOL_PAD:(k + 1) * POOL_PAD] = (1.0 / cnt)[:, None]
    return pl.pallas_call(
        functools.partial(_pool_body, n=n),
        out_shape=jax.ShapeDtypeStruct(pool_in.shape, BF16),
        grid=(nb, c // cb),
        in_specs=[pl.BlockSpec((1, n, cb, POOL_PADDED), lambda b, j: (b, 0, j, 0)),
                  pl.BlockSpec((n, 1, POOL_PADDED), lambda b, j: (0, 0, 0), pipeline_mode=pl.Buffered(1))],
        out_specs=pl.BlockSpec((1, n, cb, POOL_PADDED), lambda b, j: (b, 0, j, 0)),
        compiler_params=_params(2),
        name="pool_branch",
    )(pool_in, jnp.asarray(inv))


def _merge_body(x_ref, ys_ref, uc_ref, pp_ref, gt_ref, mg_ref, ps_ref, lg_ref, lb_ref, wss_ref, wcv_ref, wpl_ref,
                wo_ref, o_ref, *, cv_transposed):
    d = D_MODEL
    y_ssd = _dot(ys_ref[...], wss_ref[0])
    if cv_transposed:
        v = uc_ref[...].astype(F32)
        mu = jnp.mean(v, axis=0, keepdims=True)
        dv = v - mu
        var = jnp.mean(dv * dv, axis=0, keepdims=True)
        reps = v.shape[1] // LANES
        yn = dv * lax.rsqrt(var + EPS) * jnp.tile(lg_ref[0], (1, reps)) + jnp.tile(lb_ref[0], (1, reps))
        y_cv = lax.dot_general(_silu(yn).astype(BF16), wcv_ref[0], (((0,), (0,)), ((), ())),
                               preferred_element_type=F32)
    else:
        y_cv = _dot(uc_ref[...], wcv_ref[0])
    y_pool = jnp.concatenate(
        [_dot(pp_ref[:, k * POOL_PAD:(k + 1) * POOL_PAD], wpl_ref[0, k]) for k in range(POOL_GROUPS)], axis=1)
    y_pool = y_pool * ps_ref[0]
    merged = (_sigmoid(gt_ref[:, 0:d].astype(F32)) * y_ssd
              + _sigmoid(gt_ref[:, d:2 * d].astype(F32)) * y_cv
              + _sigmoid(gt_ref[:, 2 * d:3 * d].astype(F32)) * y_pool)
    o_ref[...] = x_ref[...] + mg_ref[0, 0] * _dot(merged.astype(BF16), wo_ref[0])


def _merge(x, ys, uc, pp, gates, l, mods, row_of_tile, prm, tm, cv_transposed):
    n, d = x.shape
    tok = lambda w: pl.BlockSpec((tm, w), lambda i: (i, 0))
    uc_spec = pl.BlockSpec((CONV_DIM, tm), lambda i: (0, i)) if cv_transposed else tok(CONV_DIM)
    names = ("pool_scale", "ln_g_rows", "ln_b_rows", "w_ssd", "w_cv", "w_pool", "w_out")
    return pl.pallas_call(
        functools.partial(_merge_body, cv_transposed=cv_transposed),
        out_shape=jax.ShapeDtypeStruct((n, d), F32),
        grid=(n // tm,),
        in_specs=[tok(d), tok(SSD_INNER), uc_spec, tok(POOL_PADDED), tok(3 * d), _mod_spec(l, 2, row_of_tile)]
                 + [_layer_spec(prm[k], l) for k in names],
        out_specs=tok(d),
        compiler_params=_params(1),
        name="merge",
    )(x, ys, uc, pp, gates, mods, *[prm[k] for k in names])


def _ffn_body(x_ref, g_ref, sh_ref, sc_ref, gt_ref, wg_ref, wu_ref, wd_ref, fg_ref, o_ref, *, final, hid_chunk):
    x = x_ref[...]
    hb = _norm_mod(x, g_ref[0], sh_ref[0, 0], sc_ref[0, 0]).astype(BF16)
    acc = jnp.zeros(x.shape, F32)
    for j in range(0, FFN_HIDDEN, hid_chunk):
        je = min(j + hid_chunk, FFN_HIDDEN)
        gate = _dot(hb, wg_ref[0, :, j:je])
        up = _dot(hb, wu_ref[0, :, j:je])
        acc = acc + _dot((_silu(gate) * up).astype(BF16), wd_ref[0, j:je, :])
    out = x + gt_ref[0, 0] * acc
    if final:
        ms = jnp.mean(out * out, axis=-1, keepdims=True)
        out = (out * lax.rsqrt(ms + EPS)) * fg_ref[...]
    o_ref[...] = out


def _ffn(x, l, norm_g, mods, row_of_tile, w_gate, w_up, w_down, final_g, final, tm):
    n, d = x.shape
    return pl.pallas_call(
        functools.partial(_ffn_body, final=final, hid_chunk=6 * MXU_DIM),
        out_shape=jax.ShapeDtypeStruct((n, d), F32),
        grid=(n // tm,),
        in_specs=[pl.BlockSpec((tm, d), lambda i: (i, 0)), _layer_spec(norm_g, l),
                  _mod_spec(l, 3, row_of_tile), _mod_spec(l, 4, row_of_tile), _mod_spec(l, 5, row_of_tile),
                  _layer_spec(w_gate, l), _layer_spec(w_up, l), _layer_spec(w_down, l),
                  pl.BlockSpec((1, d), lambda i: (0, 0), pipeline_mode=pl.Buffered(1))],
        out_specs=pl.BlockSpec((tm, d), lambda i: (i, 0)),
        compiler_params=_params(1),
        name="ffn",
    )(x, norm_g, mods, mods, mods, w_gate, w_up, w_down, final_g.reshape(1, d))


def _split_w_in(w):
    o = np.cumsum([0, SSD_INNER, SSD_XBC, 2 * SSD_HEADS, 2 * CONV_DIM, POOL_DIM, 3 * D_MODEL])
    z, xbc, dt, cv, pool, gates = [w[:, :, o[i]:o[i + 1]] for i in range(6)]
    lane_pad = ((0, 0), (0, 0), (0, LANES - SSD_HEADS))
    dt = jnp.concatenate([jnp.pad(dt[:, :, :SSD_HEADS], lane_pad), jnp.pad(dt[:, :, SSD_HEADS:], lane_pad)], axis=2)
    nl, d = w.shape[:2]
    pool = jnp.pad(pool.reshape(nl, d, POOL_GROUPS, POOL_GROUP_DIM),
                   ((0, 0), (0, 0), (0, 0), (0, POOL_PAD - POOL_GROUP_DIM))).reshape(nl, d, POOL_PADDED)
    cv_t = jnp.swapaxes(cv, 1, 2)
    return [v.astype(BF16) for v in (z, xbc, dt, cv, cv_t, pool, gates)]


def kernel(x, c, ctx, c_ctx, w_ada, b_ada, norm1_g, norm2_g, w_in, ssd_conv_w, ssd_conv_b, ssd_a_log,
           ssd_dt_bias, ssd_d, ssd_norm_g, ssd_w_out, cv_dw_w, cv_dw_b, cv_ln_g, cv_ln_b, cv_w_out, pool_w,
           pool_scale, w_out, ffn_w_gate, ffn_w_up, ffn_w_down, final_g):
    nb, t, d = x.shape
    tc = ctx.shape[1]
    depth = w_in.shape[0]
    rows = t // GRID_W

    n_cond = -(-(nb + 1) // 8) * 8
    cc = jnp.concatenate([c, c_ctx[None, :], jnp.zeros((n_cond - nb - 1, d), F32)], axis=0)
    mods = _ada(cc, w_ada, b_ada)

    tm_lat = min(512, t)
    tm_ctx = min(256, tc)
    ts_lat = min(512, t)
    ts_ctx = min(256, tc)
    cb_lat = min(16, GRID_W)
    lat_row = lambda i: i // (t // tm_lat)
    ctx_row = lambda i: nb

    w_z, w_xbc, w_dt, w_cv, w_cv_t, w_pool, w_gates = _split_w_in(w_in)
    w_fg, w_fu, w_fd = (v.astype(BF16) for v in (ffn_w_gate, ffn_w_up, ffn_w_down))
    row3 = lambda v: v.reshape(depth, 1, v.shape[-1])
    norm1, norm2 = row3(norm1_g), row3(norm2_g)
    ln_rows = lambda v: jnp.broadcast_to(v[:, :, None], (depth, CONV_DIM, LANES))
    merge_prm = dict(
        pool_scale=row3(pool_scale), ln_g_rows=ln_rows(cv_ln_g), ln_b_rows=ln_rows(cv_ln_b),
        w_ssd=ssd_w_out.astype(BF16), w_cv=cv_w_out.astype(BF16),
        w_pool=jnp.pad(pool_w, ((0, 0), (0, 0), (0, POOL_PAD - POOL_GROUP_DIM), (0, 0))).astype(BF16),
        w_out=w_out.astype(BF16))
    ssd_prm = _ssd_params(ssd_a_log, ssd_dt_bias, ssd_d, ssd_norm_g)
    conv5 = (jnp.pad(ssd_conv_w, ((0, 0), (0, 8 - SSD_CONV), (0, 0))), row3(ssd_conv_b))
    cv_vpu_prm = (jnp.pad(cv_dw_w, ((0, 0), (0, 32 - CONV_WIDTH), (0, 0))), row3(cv_dw_b), row3(cv_ln_g), row3(cv_ln_b))
    toe = _toeplitz_pairs(cv_dw_w)
    cv_bias = jnp.broadcast_to(cv_dw_b.reshape(-1)[:, None, None], (depth * CONV_DIM, 1, LANES))

    xl = x.reshape(nb * t, d)
    xc = ctx.reshape(nb * tc, d)
    zero_state = jnp.zeros((nb, 2, SSD_STATE, SSD_INNER), F32)

    for l in range(depth):
        last = l == depth - 1

        if last:
            xbc_c, dt_c = _inproj(xc, l, norm1, mods, ctx_row, [w_xbc, w_dt], [BF16, F32], tm_ctx, tc, 0, *conv5)
            z_c = jnp.zeros((nb, tc, SSD_INNER), BF16)
        else:
            z_c, xbc_c, dt_c, cv_c, pool_c, gates_c = _inproj(
                xc, l, norm1, mods, ctx_row, [w_z, w_xbc, w_dt, w_cv, w_pool, w_gates],
                [BF16, BF16, F32, BF16, F32, BF16], tm_ctx, tc, 1, *conv5)
            z_c = z_c.reshape(nb, tc, SSD_INNER)
        y_c, ctx_state = _ssd(xbc_c.reshape(nb, tc, SSD_XBC), dt_c.reshape(nb, tc, DT_PAD), z_c, l, ssd_prm,
                              zero_state, ts_ctx)
        if not last:
            uc_c = _conv_branch(cv_c, tc, l, *cv_vpu_prm)
            pp_c = _pool_branch(pool_c.reshape(nb, tc, 1, POOL_PADDED), tc, 1).reshape(-1, POOL_PADDED)
            xc = _merge(xc, y_c.reshape(nb * tc, SSD_INNER), uc_c, pp_c, gates_c, l, mods, ctx_row, merge_prm,
                        tm_ctx, False)
            xc = _ffn(xc, l, norm2, mods, ctx_row, w_fg, w_fu, w_fd, final_g, False, tm_ctx)

        z_l, xbc_l, dt_l, cv_l, pool_l, gates_l = _inproj(
            xl, l, norm1, mods, lat_row, [w_z, w_xbc, w_dt, w_cv_t, w_pool, w_gates],
            [BF16, BF16, F32, BF16, BF16, BF16], tm_lat, t, 1, *conv5, (False, False, False, True, False, False))
        y_l, _ = _ssd(xbc_l.reshape(nb, t, SSD_XBC), dt_l.reshape(nb, t, DT_PAD), z_l.reshape(nb, t, SSD_INNER),
                      l, ssd_prm, ctx_state, ts_lat)
        v_l = _conv_branch_mxu(cv_l, l, toe, cv_bias)
        pp_l = _pool_branch(pool_l.reshape(nb, rows, GRID_W, POOL_PADDED), rows, cb_lat).reshape(-1, POOL_PADDED)
        xl = _merge(xl, y_l.reshape(nb * t, SSD_INNER), v_l, pp_l, gates_l, l, mods, lat_row, merge_prm,
                    tm_lat, True)
        xl = _ffn(xl, l, norm2, mods, lat_row, w_fg, w_fu, w_fd, final_g, last, tm_lat)

    return xl.reshape(nb, t, d)
```

```python
import functools

import numpy as np
import jax
import jax.numpy as jnp
from jax import lax
from jax.experimental import pallas as pl
from jax.experimental.pallas import tpu as pltpu

F32 = jnp.float32
BF16 = jnp.bfloat16

D_MODEL = 1024
GRID_W = 64
EPS = 1e-6
N_MOD = 6
SSD_HEADS = 16
SSD_HEAD_DIM = 64
SSD_INNER = SSD_HEADS * SSD_HEAD_DIM
SSD_GROUPS = 2
SSD_STATE = 128
SSD_CONV = 5
SSD_XBC = SSD_INNER + 2 * SSD_GROUPS * SSD_STATE
CONV_DIM = 768
CONV_WIDTH = 31
POOL_DIM = 768
POOL_WINDOWS = (2, 4, 8, 16)
POOL_GROUPS = len(POOL_WINDOWS)
POOL_GROUP_DIM = POOL_DIM // POOL_GROUPS
POOL_OUT_DIM = D_MODEL // POOL_GROUPS
FFN_HIDDEN = 2816

LANES = 128
MXU_DIM = 256
CHUNK = 128
LOG2E = 1.4426950408889634
POOL_PAD = 256
POOL_PADDED = POOL_GROUPS * POOL_PAD
DT_PAD = 2 * LANES
HALO = 16
CV_PAD = 16
VMEM_LIMIT = 56 * 1024 * 1024


def _layer_spec(arr, l):
    shape = arr.shape[1:]
    nd = len(shape)
    return pl.BlockSpec((1,) + shape, lambda *_: (l,) + (0,) * nd, pipeline_mode=pl.Buffered(1))


def _mod_spec(l, k, row_of_tile):
    return pl.BlockSpec((1, 1, 1, D_MODEL), lambda i: (l, row_of_tile(i), 0, k))


def _params(n_axes):
    return pltpu.CompilerParams(dimension_semantics=("arbitrary",) * n_axes, vmem_limit_bytes=VMEM_LIMIT)


def _sigmoid(v):
    return 0.5 * jnp.tanh(0.5 * v) + 0.5


def _silu(v):
    return v * _sigmoid(v)


def _dot(a, b):
    return jnp.dot(a, b, preferred_element_type=F32)


def _dot_nt(a, b):
    return lax.dot_general(a, b, (((1,), (1,)), ((), ())), preferred_element_type=F32)


def _ada_body(c_ref, w_ref, b_ref, o_ref):
    s = _silu(c_ref[...])
    o_ref[0] = jnp.dot(s, w_ref[0], preferred_element_type=F32,
                       precision=lax.Precision.HIGHEST) + b_ref[0]


def _ada(cc, w_ada, b_ada):
    nl, d, nm = w_ada.shape
    r = cc.shape[0]
    tn = 1024
    return pl.pallas_call(
        _ada_body,
        out_shape=jax.ShapeDtypeStruct((nl, r, nm), F32),
        grid=(nl, nm // tn),
        in_specs=[pl.BlockSpec((r, d), lambda l, j: (0, 0)),
                  pl.BlockSpec((1, d, tn), lambda l, j: (l, 0, j)),
                  pl.BlockSpec((1, 1, tn), lambda l, j: (l, 0, j))],
        out_specs=pl.BlockSpec((1, r, tn), lambda l, j: (l, 0, j)),
        compiler_params=_params(2),
        name="adaln",
    )(cc, w_ada, b_ada.reshape(nl, 1, nm)).reshape(nl, r, 1, nm)


def _norm_mod(x, g, shift, scale):
    ms = jnp.mean(x * x, axis=-1, keepdims=True)
    h = (x * lax.rsqrt(ms + EPS)) * g
    return h * (1.0 + scale) + shift


def _inproj_body(*refs, n_arr, segs, col_chunk, conv_seg, tiles_per_seq):
    x_ref, xp_ref, xn_ref, g_ref, sh_ref, sc_ref, cw_ref, cb_ref = refs[:8]
    w_refs = refs[8:8 + n_arr]
    o_refs = refs[8 + n_arr:]
    tm = x_ref.shape[0]
    norm = lambda v: _norm_mod(v, g_ref[0], sh_ref[0, 0], sc_ref[0, 0]).astype(BF16)
    hb = norm(x_ref[...])
    pos = pl.program_id(0) % tiles_per_seq
    h_prev = jnp.where(pos > 0, norm(xp_ref[...]), jnp.zeros((HALO, x_ref.shape[1]), BF16))
    h_next = jnp.where(pos < tiles_per_seq - 1, norm(xn_ref[...]), jnp.zeros((HALO, x_ref.shape[1]), BF16))
    hb_ext = jnp.concatenate([h_prev, hb, h_next], axis=0)
    for seg, ((arr, off, n, tr), o_ref) in enumerate(zip(segs, o_refs)):
        w_ref = w_refs[arr]
        step = min(col_chunk, n)
        for j in range(0, n, step):
            if tr:
                o_ref[j:j + step, :] = _dot_nt(w_ref[0, off + j:off + j + step, :], hb).astype(o_ref.dtype)
            elif seg == conv_seg:
                r = _dot(hb_ext, w_ref[0, :, off + j:off + j + step])
                rows = tm + 2 * HALO
                acc = jnp.broadcast_to(cb_ref[0, :, j:j + step], (tm, step))
                for k in range(SSD_CONV):
                    sh = (SSD_CONV // 2 - k) % rows
                    rk = r if sh == 0 else pltpu.roll(r, sh, axis=0)
                    acc = acc + rk[HALO:HALO + tm] * cw_ref[0, k:k + 1, j:j + step]
                hv = 0.5 * acc
                o_ref[:, j:j + step] = (hv + hv * jnp.tanh(hv)).astype(o_ref.dtype)
            else:
                o_ref[:, j:j + step] = _dot(hb, w_ref[0, :, off + j:off + j + step]).astype(o_ref.dtype)


def _inproj(x, l, norm_g, mods, row_of_tile, arrays, segs, out_dtypes, tm, seq_len, conv_seg, conv_w, conv_b):
    n, d = x.shape
    out_shape, out_specs = [], []
    for (_, _, cols, tr), dt in zip(segs, out_dtypes):
        if tr:
            out_shape.append(jax.ShapeDtypeStruct((cols, n), dt))
            out_specs.append(pl.BlockSpec((cols, tm), lambda i: (0, i)))
        else:
            out_shape.append(jax.ShapeDtypeStruct((n, cols), dt))
            out_specs.append(pl.BlockSpec((tm, cols), lambda i: (i, 0)))
    hb = tm // HALO
    return pl.pallas_call(
        functools.partial(_inproj_body, n_arr=len(arrays), segs=tuple(segs), col_chunk=512, conv_seg=conv_seg,
                          tiles_per_seq=seq_len // tm),
        out_shape=out_shape,
        grid=(n // tm,),
        in_specs=[pl.BlockSpec((tm, d), lambda i: (i, 0)),
                  pl.BlockSpec((HALO, d), lambda i: (jnp.maximum(i * hb - 1, 0), 0)),
                  pl.BlockSpec((HALO, d), lambda i: (jnp.minimum((i + 1) * hb, n // HALO - 1), 0)),
                  _layer_spec(norm_g, l), _mod_spec(l, 0, row_of_tile), _mod_spec(l, 1, row_of_tile),
                  _layer_spec(conv_w, l), _layer_spec(conv_b, l)]
                 + [_layer_spec(w, l) for w in arrays],
        out_specs=out_specs,
        compiler_params=_params(1),
        name="inproj",
    )(x, x, x, norm_g, mods, mods, conv_w, conv_b, *arrays)


def _split3(v):
    hi = v.astype(BF16)
    r1 = v - hi.astype(F32)
    mid = r1.astype(BF16)
    lo = (r1 - mid.astype(F32)).astype(BF16)
    return hi, mid, lo


def _ssd_body(xm_ref, dt_ref, z_ref, alog_ref, dtb_ref, dsk_ref,
              ng_ref, init_ref, y_ref, st_ref, yf_all, pt_ref, ec_ref, *, n_steps, ts):
    s = pl.program_id(1)
    fwd = s < n_steps
    c = jnp.where(fwd, s, 2 * n_steps - 1 - s)
    base = pl.multiple_of(c * ts, ts)
    n_sub = ts // CHUNK
    nh = SSD_HEADS
    inner = SSD_INNER
    half_grp = SSD_GROUPS * SSD_STATE
    pairs_per_group = SSD_HEADS // SSD_GROUPS // 2

    @pl.when(jnp.logical_or(s == 0, s == n_steps))
    def _():
        st_ref[...] = init_ref[...]

    @pl.when(fwd)
    def _():
        yf_all[pl.ds(base, ts), :] = xm_ref[0, :, :inner].astype(F32) * dsk_ref[0]

    li = lax.broadcasted_iota(jnp.int32, (CHUNK, CHUNK), 0)
    si = lax.broadcasted_iota(jnp.int32, (CHUNK, CHUNK), 1)
    tri = jnp.where(fwd, li - si, si - li) >= 0
    first_half = si < SSD_HEAD_DIM
    first_half_row = first_half[0:1]

    dtr = dt_ref[0]
    dtr_t = jnp.concatenate([dtr[j * CHUNK:(j + 1) * CHUNK].T[:nh] for j in range(n_sub)], axis=0)
    dtr_t = dtr_t + jnp.tile(dtb_ref[0, 0], (n_sub, 1))
    dt_t = jnp.maximum(dtr_t, 0.0) + jnp.log(1.0 + jnp.exp(-jnp.abs(dtr_t)))
    a2_t = dt_t * jnp.tile(-jnp.exp(alog_ref[0, 0]) * LOG2E, (n_sub, 1))
    order = jnp.where(tri, 1.0, 0.0).astype(BF16)
    pieces = _split3(a2_t)
    e2_t = sum(_dot_nt(p, order) for p in pieces)
    tot2 = jnp.sum(a2_t, axis=1, keepdims=True)
    er2_t = e2_t - jnp.log(dt_t) * LOG2E
    pt_ref[0] = er2_t
    pt_ref[1] = jnp.exp2(tot2 - er2_t)
    pt_ref[2] = jnp.broadcast_to(jnp.exp2(tot2), e2_t.shape)
    zpad = jnp.zeros((CHUNK - nh, CHUNK), BF16)
    for j in range(n_sub):
        ec_ref[j * CHUNK:(j + 1) * CHUNK, :] = sum(
            _dot_nt(order, jnp.concatenate([p[j * nh:(j + 1) * nh], zpad], axis=0)) for p in pieces)

    for j in range(n_sub):
        jj = jnp.where(fwd, j, n_sub - 1 - j)
        loc = pl.multiple_of(jj * CHUNK, CHUNK)
        off = pl.multiple_of(base + loc, CHUNK)
        hrow = pl.multiple_of(jj * nh, nh)
        xc = xm_ref[0, pl.ds(loc, CHUNK), :]
        er2 = pt_ref[0, pl.ds(hrow, nh), :]
        w_t = pt_ref[1, pl.ds(hrow, nh), :]
        etot = pt_ref[2, pl.ds(hrow, nh), :]
        e_col = ec_ref[pl.ds(loc, CHUNK), :]
        y_pairs = []
        for g in range(SSD_GROUPS):
            b_g = xc[:, inner + g * SSD_STATE: inner + (g + 1) * SSD_STATE]
            c_g = xc[:, inner + half_grp + g * SSD_STATE: inner + half_grp + (g + 1) * SSD_STATE]
            cb = _dot_nt(c_g, b_g).astype(BF16)
            b_t = b_g.astype(F32).T.astype(BF16)
            gcols = slice(g * inner // SSD_GROUPS, (g + 1) * inner // SSD_GROUPS)
            y_carry = _dot(c_g, st_ref[0, 0, :, gcols].astype(BF16))
            for pr in range(pairs_per_group):
                pair = g * pairs_per_group + pr
                cols = slice(pair * LANES, (pair + 1) * LANES)
                xp = xc[:, cols]
                zero = jnp.zeros_like(xp)
                x12 = jnp.concatenate([jnp.where(first_half, xp, zero), jnp.where(first_half, zero, xp)], axis=0)
                ms, bw, dec = [], [], []
                for h in (2 * pair, 2 * pair + 1):
                    ecol = jnp.broadcast_to(e_col[:, h:h + 1], (CHUNK, CHUNK))
                    ms.append(jnp.exp2(jnp.where(tri, ecol - er2[h:h + 1, :], -1e30)).astype(BF16) * cb)
                    dec.append(jnp.exp2(ecol))
                    w_rows = jnp.broadcast_to(w_t[h:h + 1, :], (2 * 8, CHUNK)).astype(BF16)
                    bw.append(b_t * jnp.tile(w_rows, (SSD_STATE // (2 * 8), 1)))
                both = _dot(jnp.concatenate([jnp.concatenate(ms, axis=1), jnp.concatenate(bw, axis=1)], axis=0), x12)
                y_pairs.append(both[:CHUNK] + jnp.where(first_half, dec[0], dec[1]) * y_carry[:, pr * LANES:(pr + 1) * LANES])
                erow = jnp.where(first_half_row, etot[2 * pair:2 * pair + 1, :], etot[2 * pair + 1:2 * pair + 2, :])
                st_ref[0, 0, :, cols] = st_ref[0, 0, :, cols] * erow + both[CHUNK:]
        yf_all[pl.ds(off, CHUNK), :] = yf_all[pl.ds(off, CHUNK), :] + jnp.concatenate(y_pairs, axis=1)

    @pl.when(jnp.logical_not(fwd))
    def _():
        y = yf_all[pl.ds(base, ts), :] * _silu(z_ref[0].astype(F32))
        gw = inner // SSD_GROUPS
        for g in range(SSD_GROUPS):
            yg = y[:, g * gw:(g + 1) * gw]
            ms = jnp.mean(yg * yg, axis=-1, keepdims=True)
            yn = (yg * lax.rsqrt(ms + EPS)) * ng_ref[0, :, g * gw:(g + 1) * gw]
            y_ref[0, :, g * gw:(g + 1) * gw] = yn.astype(y_ref.dtype)


def _ssd(xbc, dt, z, l, prm, init, ts):
    nb, t, _ = xbc.shape
    n_steps = t // ts

    def tok_out(s):
        return jnp.where(s < n_steps, n_steps - 1, 2 * n_steps - 1 - s)

    def tok_dt(s):
        return jnp.where(s < n_steps, s, 2 * n_steps - 1 - s)

    def phase(s):
        return jnp.where(s < n_steps, 0, 1)

    head_spec = pl.BlockSpec((1, 1, SSD_HEADS, LANES), lambda b, s: (l, phase(s), 0, 0))
    return pl.pallas_call(
        functools.partial(_ssd_body, n_steps=n_steps, ts=ts),
        out_shape=[jax.ShapeDtypeStruct((nb, t, SSD_INNER), BF16),
                   jax.ShapeDtypeStruct((nb, 2, SSD_STATE, SSD_INNER), F32)],
        grid=(nb, 2 * n_steps),
        in_specs=[
            pl.BlockSpec((1, ts, SSD_XBC), lambda b, s: (b, tok_dt(s), 0)),
            pl.BlockSpec((1, ts, LANES), lambda b, s: (b, tok_dt(s), phase(s))),
            pl.BlockSpec((1, ts, SSD_INNER), lambda b, s: (b, tok_out(s), 0)),
            head_spec,
            head_spec,
            _layer_spec(prm["d_full"], l),
            _layer_spec(prm["norm_g"], l),
            pl.BlockSpec((1, 1, SSD_STATE, SSD_INNER), lambda b, s: (b, phase(s), 0, 0)),
        ],
        out_specs=[
            pl.BlockSpec((1, ts, SSD_INNER), lambda b, s: (b, tok_out(s), 0)),
            pl.BlockSpec((1, 1, SSD_STATE, SSD_INNER), lambda b, s: (b, phase(s), 0, 0)),
        ],
        scratch_shapes=[pltpu.VMEM((t, SSD_INNER), F32),
                        pltpu.VMEM((3, (ts // CHUNK) * SSD_HEADS, CHUNK), F32), pltpu.VMEM((ts, LANES), F32)],
        compiler_params=_params(2),
        name="ssd_scan",
    )(xbc, dt, z, prm["a_log"], prm["dt_bias"], prm["d_full"], prm["norm_g"], init)


def _ssd_params(a_log, dt_bias, d_skip, norm_g):
    nl = a_log.shape[0]
    rep = lambda v: jnp.broadcast_to(v[:, :, :, None], (nl, 2, SSD_HEADS, LANES))
    return dict(
        a_log=rep(a_log), dt_bias=rep(dt_bias),
        d_full=jnp.repeat(d_skip, SSD_HEAD_DIM, axis=1).reshape(nl, 1, SSD_INNER),
        norm_g=norm_g.reshape(nl, 1, SSD_INNER))


def _cv_body(x_ref, w_ref, b_ref, lg_ref, lb_ref, o_ref, pad_ref, *, width, n_rows):
    zeros = jnp.zeros((CV_PAD, CONV_DIM), F32)
    pad_ref[0:CV_PAD, :] = zeros
    pad_ref[CV_PAD + width:, :] = zeros
    n_q = (CONV_WIDTH + 8) // 8

    def row_body(r, carry):
        xr = x_ref[r]
        u = xr[:, :CONV_DIM].astype(F32) * _sigmoid(xr[:, CONV_DIM:].astype(F32))
        pad_ref[CV_PAD:CV_PAD + width, :] = u
        cols = []
        for lt in range(CONV_DIM // LANES):
            ls = slice(lt * LANES, (lt + 1) * LANES)
            acc = jnp.broadcast_to(b_ref[0, :, ls], (width, LANES))
            for rr in range(8):
                shifted = pad_ref[rr:rr + width + 8 * (n_q - 1), ls]
                for q in range(n_q):
                    k = 8 * q + rr - (CV_PAD - CONV_WIDTH // 2)
                    if 0 <= k < CONV_WIDTH:
                        acc = acc + shifted[8 * q:8 * q + width] * w_ref[0, k:k + 1, ls]
            cols.append(acc)
        v = jnp.concatenate(cols, axis=1)
        mu = jnp.mean(v, axis=-1, keepdims=True)
        dv = v - mu
        var = jnp.mean(dv * dv, axis=-1, keepdims=True)
        yn = dv * lax.rsqrt(var + EPS) * lg_ref[0] + lb_ref[0]
        o_ref[r] = _silu(yn).astype(o_ref.dtype)
        return carry

    lax.fori_loop(0, n_rows, row_body, 0)


def _conv_branch(cv, width, l, dw_w, dw_b, ln_g, ln_b):
    n = cv.shape[0]
    rows = n // width
    n_rows = max(1, min(rows, 1024 // width))
    x3 = cv.reshape(rows, width, 2 * CONV_DIM)
    out = pl.pallas_call(
        functools.partial(_cv_body, width=width, n_rows=n_rows),
        out_shape=jax.ShapeDtypeStruct((rows, width, CONV_DIM), BF16),
        grid=(rows // n_rows,),
        in_specs=[pl.BlockSpec((n_rows, width, 2 * CONV_DIM), lambda i: (i, 0, 0)),
                  _layer_spec(dw_w, l), _layer_spec(dw_b, l), _layer_spec(ln_g, l), _layer_spec(ln_b, l)],
        out_specs=pl.BlockSpec((n_rows, width, CONV_DIM), lambda i: (i, 0, 0)),
        scratch_shapes=[pltpu.VMEM((width + 2 * CV_PAD, CONV_DIM), F32)],
        compiler_params=_params(1),
        name="conv_branch",
    )(x3, dw_w, dw_b, ln_g, ln_b)
    return out.reshape(n, CONV_DIM)


def _toeplitz_pairs(dw_w):
    half = CONV_WIDTH // 2
    s_idx, t_idx = np.meshgrid(np.arange(LANES), np.arange(LANES), indexing="ij")
    same_row = (s_idx // GRID_W) == (t_idx // GRID_W)
    sel = np.stack([(s_idx - t_idx + half == k) & same_row for k in range(CONV_WIDTH)]).astype(np.float32)
    taps = jnp.swapaxes(dw_w, 1, 2).reshape(-1, CONV_WIDTH).astype(BF16)
    return jnp.einsum("ck,kst->cst", taps, jnp.asarray(sel, BF16), preferred_element_type=BF16)


def _cvmx_body(a_ref, b_ref, t_ref, bias_ref, o_ref, *, n_ch):
    def ch_body(c, carry):
        u = a_ref[c].astype(F32) * _sigmoid(b_ref[c].astype(F32))
        o_ref[c] = (_dot(u.astype(BF16), t_ref[c]) + bias_ref[c]).astype(o_ref.dtype)
        return carry

    lax.fori_loop(0, n_ch, ch_body, 0, unroll=8)


def _conv_branch_mxu(cv_t, l, toe, bias):
    n = cv_t.shape[1]
    sr = n // LANES
    rb = min(256, sr)
    cbk = 64
    x3 = cv_t.reshape(2 * CONV_DIM, sr, LANES)
    nblk = CONV_DIM // cbk
    out = pl.pallas_call(
        functools.partial(_cvmx_body, n_ch=cbk),
        out_shape=jax.ShapeDtypeStruct((CONV_DIM, sr, LANES), BF16),
        grid=(nblk, sr // rb),
        in_specs=[pl.BlockSpec((cbk, rb, LANES), lambda c, i: (c, i, 0)),
                  pl.BlockSpec((cbk, rb, LANES), lambda c, i: (c + nblk, i, 0)),
                  pl.BlockSpec((cbk, LANES, LANES), lambda c, i: (l * nblk + c, 0, 0)),
                  pl.BlockSpec((cbk, 1, LANES), lambda c, i: (l * nblk + c, 0, 0))],
        out_specs=pl.BlockSpec((cbk, rb, LANES), lambda c, i: (c, i, 0)),
        compiler_params=_params(2),
        name="conv_branch_mxu",
    )(x3, x3, toe, bias)
    return out.reshape(CONV_DIM, n)


def _pool_body(x_ref, ic_ref, o_ref, *, n):
    max_lo = max(POOL_WINDOWS) // 2
    for k, w in enumerate(POOL_WINDOWS):
        ls = slice(k * POOL_PAD, (k + 1) * POOL_PAD)
        x = x_ref[0, :, :, ls].astype(F32)
        zeros = jnp.zeros((max_lo,) + x.shape[1:], F32)
        run = jnp.concatenate([zeros, x, zeros], axis=0)
        span = 1
        while span < w:
            run = run[:-span] + run[span:]
            span *= 2
        lo = w // 2
        total = run[max_lo - lo:max_lo - lo + n]
        o_ref[0, :, :, ls] = (total * ic_ref[:, :, ls] - x).astype(o_ref.dtype)


def _pool_branch(pool_in, n, cb):
    nb, _, c, _ = pool_in.shape
    pos = np.arange(n)
    inv = np.zeros((n, 1, POOL_PADDED), np.float32)
    for k, w in enumerate(POOL_WINDOWS):
        lo = w // 2
        cnt = np.clip(pos + (w - 1 - lo) + 1, 0, n) - np.clip(pos - lo, 0, n)
        inv[:, 0, k * POOL_PAD:(k + 1) * POOL_PAD] = (1.0 / cnt)[:, None]
    return pl.pallas_call(
        functools.partial(_pool_body, n=n),
        out_shape=jax.ShapeDtypeStruct(pool_in.shape, BF16),
        grid=(nb, c // cb),
        in_specs=[pl.BlockSpec((1, n, cb, POOL_PADDED), lambda b, j: (b, 0, j, 0)),
                  pl.BlockSpec((n, 1, POOL_PADDED), lambda b, j: (0, 0, 0), pipeline_mode=pl.Buffered(1))],
        out_specs=pl.BlockSpec((1, n, cb, POOL_PADDED), lambda b, j: (b, 0, j, 0)),
        compiler_params=_params(2),
        name="pool_branch",
    )(pool_in, jnp.asarray(inv))


def _pool_seq_body(x_ref, band_ref, ic_ref, o_ref):
    for k in range(POOL_GROUPS):
        ls = slice(k * POOL_PAD, (k + 1) * POOL_PAD)
        x = x_ref[:, ls]
        o_ref[:, ls] = (_dot(band_ref[k], x) * ic_ref[k] - x.astype(F32)).astype(o_ref.dtype)


def _pool_seq(pool_in, n):
    pos = np.arange(n)
    band = np.zeros((POOL_GROUPS, n, n), np.float32)
    inv = np.zeros((POOL_GROUPS, n, POOL_PAD), np.float32)
    for k, w in enumerate(POOL_WINDOWS):
        lo = w // 2
        inside = (pos[None, :] >= pos[:, None] - lo) & (pos[None, :] <= pos[:, None] + (w - 1 - lo))
        band[k] = inside
        inv[k] = (1.0 / inside.sum(axis=1))[:, None]
    whole = lambda shape: pl.BlockSpec(shape, lambda b: (0,) * len(shape), pipeline_mode=pl.Buffered(1))
    return pl.pallas_call(
        _pool_seq_body,
        out_shape=jax.ShapeDtypeStruct(pool_in.shape, BF16),
        grid=(pool_in.shape[0] // n,),
        in_specs=[pl.BlockSpec((n, POOL_PADDED), lambda b: (b, 0)), whole(band.shape), whole(inv.shape)],
        out_specs=pl.BlockSpec((n, POOL_PADDED), lambda b: (b, 0)),
        compiler_params=_params(1),
        name="pool_seq",
    )(pool_in, jnp.asarray(band, BF16), jnp.asarray(inv))


def _merge_body(x_ref, ys_ref, uc_ref, pp_ref, gt_ref, mg_ref, ps_ref, lg_ref, lb_ref, wss_ref, wcv_ref, wpl_ref,
                wo_ref, o_ref, *, cv_transposed):
    d = D_MODEL
    y_ssd = _dot(ys_ref[...], wss_ref[0])
    if cv_transposed:
        v = uc_ref[...].astype(F32)
        mu = jnp.mean(v, axis=0, keepdims=True)
        dv = v - mu
        var = jnp.mean(dv * dv, axis=0, keepdims=True)
        reps = v.shape[1] // LANES
        yn = dv * lax.rsqrt(var + EPS) * jnp.tile(lg_ref[0], (1, reps)) + jnp.tile(lb_ref[0], (1, reps))
        y_cv = lax.dot_general(_silu(yn).astype(BF16), wcv_ref[0], (((0,), (0,)), ((), ())),
                               preferred_element_type=F32)
    else:
        y_cv = _dot(uc_ref[...], wcv_ref[0])
    y_pool = jnp.concatenate(
        [_dot(pp_ref[:, k * POOL_PAD:(k + 1) * POOL_PAD], wpl_ref[0, k]) for k in range(POOL_GROUPS)], axis=1)
    y_pool = y_pool * ps_ref[0]
    merged2 = ((jnp.tanh(gt_ref[:, 0:d].astype(F32)) + 1.0) * y_ssd
               + (jnp.tanh(gt_ref[:, d:2 * d].astype(F32)) + 1.0) * y_cv
               + (jnp.tanh(gt_ref[:, 2 * d:3 * d].astype(F32)) + 1.0) * y_pool)
    o_ref[...] = x_ref[...] + (0.5 * mg_ref[0, 0]) * _dot(merged2.astype(BF16), wo_ref[0])


def _merge(x, ys, uc, pp, gates, l, mods, row_of_tile, prm, tm, cv_transposed):
    n, d = x.shape
    tok = lambda w: pl.BlockSpec((tm, w), lambda i: (i, 0))
    uc_spec = pl.BlockSpec((CONV_DIM, tm), lambda i: (0, i)) if cv_transposed else tok(CONV_DIM)
    names = ("pool_scale", "ln_g_rows", "ln_b_rows", "w_ssd", "w_cv", "w_pool", "w_out")
    return pl.pallas_call(
        functools.partial(_merge_body, cv_transposed=cv_transposed),
        out_shape=jax.ShapeDtypeStruct((n, d), F32),
        grid=(n // tm,),
        in_specs=[tok(d), tok(SSD_INNER), uc_spec, tok(POOL_PADDED), tok(3 * d), _mod_spec(l, 2, row_of_tile)]
                 + [_layer_spec(prm[k], l) for k in names],
        out_specs=tok(d),
        compiler_params=_params(1),
        name="merge",
    )(x, ys, uc, pp, gates, mods, *[prm[k] for k in names])


def _ffn_body(x_ref, g_ref, sh_ref, sc_ref, gt_ref, wg_ref, wu_ref, wd_ref, fg_ref, o_ref, *, final, hid_chunk):
    x = x_ref[...]
    hb = _norm_mod(x, g_ref[0], sh_ref[0, 0], sc_ref[0, 0]).astype(BF16)
    acc = jnp.zeros(x.shape, F32)
    for j in range(0, FFN_HIDDEN, hid_chunk):
        je = min(j + hid_chunk, FFN_HIDDEN)
        gate = _dot(hb, wg_ref[0, :, j:je])
        up = _dot(hb, wu_ref[0, :, j:je])
        acc = acc + _dot((_silu(gate) * up).astype(BF16), wd_ref[0, j:je, :])
    out = x + gt_ref[0, 0] * acc
    if final:
        ms = jnp.mean(out * out, axis=-1, keepdims=True)
        out = (out * lax.rsqrt(ms + EPS)) * fg_ref[...]
    o_ref[...] = out


def _ffn(x, l, norm_g, mods, row_of_tile, w_gate, w_up, w_down, final_g, final, tm):
    n, d = x.shape
    return pl.pallas_call(
        functools.partial(_ffn_body, final=final, hid_chunk=6 * MXU_DIM),
        out_shape=jax.ShapeDtypeStruct((n, d), F32),
        grid=(n // tm,),
        in_specs=[pl.BlockSpec((tm, d), lambda i: (i, 0)), _layer_spec(norm_g, l),
                  _mod_spec(l, 3, row_of_tile), _mod_spec(l, 4, row_of_tile), _mod_spec(l, 5, row_of_tile),
                  _layer_spec(w_gate, l), _layer_spec(w_up, l), _layer_spec(w_down, l),
                  pl.BlockSpec((1, d), lambda i: (0, 0), pipeline_mode=pl.Buffered(1))],
        out_specs=pl.BlockSpec((tm, d), lambda i: (i, 0)),
        compiler_params=_params(1),
        name="ffn",
    )(x, norm_g, mods, mods, mods, w_gate, w_up, w_down, final_g.reshape(1, d))


W_MAIN_COLS = dict(z=(0, SSD_INNER), xbc=(SSD_INNER, SSD_XBC), dt=(SSD_INNER + SSD_XBC, DT_PAD),
                   pool=(SSD_INNER + SSD_XBC + DT_PAD, POOL_PADDED),
                   gates=(SSD_INNER + SSD_XBC + DT_PAD + POOL_PADDED, 3 * D_MODEL))


def _split_w_in(w):
    o = np.cumsum([0, SSD_INNER, SSD_XBC, 2 * SSD_HEADS, 2 * CONV_DIM, POOL_DIM, 3 * D_MODEL])
    z, xbc, dt, cv, pool, gates = [w[:, :, o[i]:o[i + 1]] for i in range(6)]
    nl, d = w.shape[:2]
    lane_pad = jnp.zeros((nl, d, LANES - SSD_HEADS), w.dtype)
    group_pad = jnp.zeros((nl, d, POOL_PAD - POOL_GROUP_DIM), w.dtype)
    pool_parts = []
    for k in range(POOL_GROUPS):
        pool_parts += [pool[:, :, k * POOL_GROUP_DIM:(k + 1) * POOL_GROUP_DIM], group_pad]
    main = jnp.concatenate([z, xbc, dt[:, :, :SSD_HEADS], lane_pad, dt[:, :, SSD_HEADS:], lane_pad]
                           + pool_parts + [0.5 * gates], axis=2).astype(BF16)
    return main, cv.astype(BF16), jnp.swapaxes(cv, 1, 2).astype(BF16)


def kernel(x, c, ctx, c_ctx, w_ada, b_ada, norm1_g, norm2_g, w_in, ssd_conv_w, ssd_conv_b, ssd_a_log,
           ssd_dt_bias, ssd_d, ssd_norm_g, ssd_w_out, cv_dw_w, cv_dw_b, cv_ln_g, cv_ln_b, cv_w_out, pool_w,
           pool_scale, w_out, ffn_w_gate, ffn_w_up, ffn_w_down, final_g):
    nb, t, d = x.shape
    tc = ctx.shape[1]
    depth = w_in.shape[0]
    rows = t // GRID_W

    n_cond = -(-(nb + 1) // 8) * 8
    cc = jnp.concatenate([c, c_ctx[None, :], jnp.zeros((n_cond - nb - 1, d), F32)], axis=0)
    mods = _ada(cc, w_ada, b_ada)

    tm_lat = min(512, t)
    tm_ctx = min(256, tc)
    ts_lat = min(1024, t)
    ts_ctx = min(256, tc)
    cb_lat = min(16, GRID_W)
    lat_row = lambda i: i // (t // tm_lat)
    ctx_row = lambda i: nb

    w_main, w_cv, w_cv_t = _split_w_in(w_in)
    seg = lambda name, arr=0, tr=False: (arr,) + W_MAIN_COLS[name] + (tr,)
    w_fg, w_fu, w_fd = (v.astype(BF16) for v in (ffn_w_gate, ffn_w_up, ffn_w_down))
    row3 = lambda v: v.reshape(depth, 1, v.shape[-1])
    norm1, norm2 = row3(norm1_g), row3(norm2_g)
    ln_rows = lambda v: jnp.broadcast_to(v[:, :, None], (depth, CONV_DIM, LANES))
    merge_prm = dict(
        pool_scale=row3(pool_scale), ln_g_rows=ln_rows(cv_ln_g), ln_b_rows=ln_rows(cv_ln_b),
        w_ssd=ssd_w_out.astype(BF16), w_cv=cv_w_out.astype(BF16),
        w_pool=jnp.pad(pool_w, ((0, 0), (0, 0), (0, POOL_PAD - POOL_GROUP_DIM), (0, 0))).astype(BF16),
        w_out=w_out.astype(BF16))
    ssd_prm = _ssd_params(ssd_a_log, ssd_dt_bias, ssd_d, ssd_norm_g)
    conv5 = (jnp.pad(ssd_conv_w, ((0, 0), (0, 8 - SSD_CONV), (0, 0))), row3(ssd_conv_b))
    cv_vpu_prm = (jnp.pad(cv_dw_w, ((0, 0), (0, 32 - CONV_WIDTH), (0, 0))), row3(cv_dw_b), row3(cv_ln_g), row3(cv_ln_b))
    toe = _toeplitz_pairs(cv_dw_w)
    cv_bias = jnp.broadcast_to(cv_dw_b.reshape(-1)[:, None, None], (depth * CONV_DIM, 1, LANES))

    xl = x.reshape(nb * t, d)
    xc = ctx.reshape(nb * tc, d)
    zero_state = jnp.zeros((nb, 2, SSD_STATE, SSD_INNER), F32)

    for l in range(depth):
        last = l == depth - 1

        if last:
            xbc_c, dt_c = _inproj(xc, l, norm1, mods, ctx_row, [w_main], [seg("xbc"), seg("dt")], [BF16, F32],
                                  tm_ctx, tc, 0, *conv5)
            z_c = jnp.zeros((nb, tc, SSD_INNER), BF16)
        else:
            z_c, xbc_c, dt_c, cv_c, pool_c, gates_c = _inproj(
                xc, l, norm1, mods, ctx_row, [w_main, w_cv],
                [seg("z"), seg("xbc"), seg("dt"), (1, 0, 2 * CONV_DIM, False), seg("pool"), seg("gates")],
                [BF16, BF16, F32, BF16, BF16, BF16], tm_ctx, tc, 1, *conv5)
            z_c = z_c.reshape(nb, tc, SSD_INNER)
        y_c, ctx_state = _ssd(xbc_c.reshape(nb, tc, SSD_XBC), dt_c.reshape(nb, tc, DT_PAD), z_c, l, ssd_prm,
                              zero_state, ts_ctx)
        if not last:
            uc_c = _conv_branch(cv_c, tc, l, *cv_vpu_prm)
            pp_c = _pool_seq(pool_c, tc)
            xc = _merge(xc, y_c.reshape(nb * tc, SSD_INNER), uc_c, pp_c, gates_c, l, mods, ctx_row, merge_prm,
                        tm_ctx, False)
            xc = _ffn(xc, l, norm2, mods, ctx_row, w_fg, w_fu, w_fd, final_g, False, tm_ctx)

        z_l, xbc_l, dt_l, cv_l, pool_l, gates_l = _inproj(
            xl, l, norm1, mods, lat_row, [w_main, w_cv_t],
            [seg("z"), seg("xbc"), seg("dt"), (1, 0, 2 * CONV_DIM, True), seg("pool"), seg("gates")],
            [BF16, BF16, F32, BF16, BF16, BF16], tm_lat, t, 1, *conv5)
        y_l, _ = _ssd(xbc_l.reshape(nb, t, SSD_XBC), dt_l.reshape(nb, t, DT_PAD), z_l.reshape(nb, t, SSD_INNER),
                      l, ssd_prm, ctx_state, ts_lat)
        v_l = _conv_branch_mxu(cv_l, l, toe, cv_bias)
        pp_l = _pool_branch(pool_l.reshape(nb, rows, GRID_W, POOL_PADDED), rows, cb_lat).reshape(-1, POOL_PADDED)
        xl = _merge(xl, y_l.reshape(nb * t, SSD_INNER), v_l, pp_l, gates_l, l, mods, lat_row, merge_prm,
                    tm_lat, True)
        xl = _ffn(xl, l, norm2, mods, lat_row, w_fg, w_fu, w_fd, final_g, last, tm_lat)

    return xl.reshape(nb, t, d)
```

```python
import functools

import numpy as np
import jax
import jax.numpy as jnp
from jax import lax
from jax.experimental import pallas as pl
from jax.experimental.pallas import tpu as pltpu

F32 = jnp.float32
BF16 = jnp.bfloat16

D_MODEL = 1024
GRID_W = 64
EPS = 1e-6
N_MOD = 6
SSD_HEADS = 16
SSD_HEAD_DIM = 64
SSD_INNER = SSD_HEADS * SSD_HEAD_DIM
SSD_GROUPS = 2
SSD_STATE = 128
SSD_CONV = 5
SSD_XBC = SSD_INNER + 2 * SSD_GROUPS * SSD_STATE
CONV_DIM = 768
CONV_WIDTH = 31
POOL_DIM = 768
POOL_WINDOWS = (2, 4, 8, 16)
POOL_GROUPS = len(POOL_WINDOWS)
POOL_GROUP_DIM = POOL_DIM // POOL_GROUPS
POOL_OUT_DIM = D_MODEL // POOL_GROUPS
FFN_HIDDEN = 2816

LANES = 128
MXU_DIM = 256
CHUNK = 128
LOG2E = 1.4426950408889634
POOL_PAD = 256
POOL_PADDED = POOL_GROUPS * POOL_PAD
DT_PAD = 2 * LANES
HALO = 16
CV_PAD = 16
VMEM_LIMIT = 56 * 1024 * 1024


def _layer_spec(arr, l):
    shape = arr.shape[1:]
    nd = len(shape)
    return pl.BlockSpec((1,) + shape, lambda *_: (l,) + (0,) * nd, pipeline_mode=pl.Buffered(1))


def _mod_spec(l, k, row_of_tile):
    return pl.BlockSpec((1, 1, 1, D_MODEL), lambda i: (l, row_of_tile(i), 0, k))


def _params(n_axes):
    return pltpu.CompilerParams(dimension_semantics=("arbitrary",) * n_axes, vmem_limit_bytes=VMEM_LIMIT)


def _sigmoid(v):
    return 0.5 * jnp.tanh(0.5 * v) + 0.5


def _silu(v):
    return v * _sigmoid(v)


def _dot(a, b):
    return jnp.dot(a, b, preferred_element_type=F32)


def _dot_nt(a, b):
    return lax.dot_general(a, b, (((1,), (1,)), ((), ())), preferred_element_type=F32)


def _ada_body(c_ref, w_ref, b_ref, o_ref):
    s = _silu(c_ref[...])
    o_ref[0] = jnp.dot(s, w_ref[0], preferred_element_type=F32,
                       precision=lax.Precision.HIGHEST) + b_ref[0]


def _ada(cc, w_ada, b_ada):
    nl, d, nm = w_ada.shape
    r = cc.shape[0]
    tn = 1024
    return pl.pallas_call(
        _ada_body,
        out_shape=jax.ShapeDtypeStruct((nl, r, nm), F32),
        grid=(nl, nm // tn),
        in_specs=[pl.BlockSpec((r, d), lambda l, j: (0, 0)),
                  pl.BlockSpec((1, d, tn), lambda l, j: (l, 0, j)),
                  pl.BlockSpec((1, 1, tn), lambda l, j: (l, 0, j))],
        out_specs=pl.BlockSpec((1, r, tn), lambda l, j: (l, 0, j)),
        compiler_params=_params(2),
        name="adaln",
    )(cc, w_ada, b_ada.reshape(nl, 1, nm)).reshape(nl, r, 1, nm)


def _norm_mod(x, g, shift, scale):
    ms = jnp.mean(x * x, axis=-1, keepdims=True)
    h = (x * lax.rsqrt(ms + EPS)) * g
    return h * (1.0 + scale) + shift


def _inproj_body(*refs, n_arr, segs, col_chunk, conv_seg, tiles_per_seq):
    x_ref, xp_ref, xn_ref, g_ref, sh_ref, sc_ref, cw_ref, cb_ref = refs[:8]
    w_refs = refs[8:8 + n_arr]
    o_refs = refs[8 + n_arr:]
    tm = x_ref.shape[0]
    norm = lambda v: _norm_mod(v, g_ref[0], sh_ref[0, 0], sc_ref[0, 0]).astype(BF16)
    hb = norm(x_ref[...])
    pos = pl.program_id(0) % tiles_per_seq
    h_prev = jnp.where(pos > 0, norm(xp_ref[...]), jnp.zeros((HALO, x_ref.shape[1]), BF16))
    h_next = jnp.where(pos < tiles_per_seq - 1, norm(xn_ref[...]), jnp.zeros((HALO, x_ref.shape[1]), BF16))
    hb_ext = jnp.concatenate([h_prev, hb, h_next], axis=0)
    for seg, ((arr, off, n, tr), o_ref) in enumerate(zip(segs, o_refs)):
        w_ref = w_refs[arr]
        step = min(col_chunk, n)
        for j in range(0, n, step):
            if tr:
                o_ref[j:j + step, :] = _dot_nt(w_ref[0, off + j:off + j + step, :], hb).astype(o_ref.dtype)
            elif seg == conv_seg:
                r = _dot(hb_ext, w_ref[0, :, off + j:off + j + step])
                rows = tm + 2 * HALO
                acc = jnp.broadcast_to(cb_ref[0, :, j:j + step], (tm, step))
                for k in range(SSD_CONV):
                    sh = (SSD_CONV // 2 - k) % rows
                    rk = r if sh == 0 else pltpu.roll(r, sh, axis=0)
                    acc = acc + rk[HALO:HALO + tm] * cw_ref[0, k:k + 1, j:j + step]
                hv = 0.5 * acc
                o_ref[:, j:j + step] = (hv + hv * jnp.tanh(hv)).astype(o_ref.dtype)
            else:
                o_ref[:, j:j + step] = _dot(hb, w_ref[0, :, off + j:off + j + step]).astype(o_ref.dtype)


def _inproj(x, l, norm_g, mods, row_of_tile, arrays, segs, out_dtypes, tm, seq_len, conv_seg, conv_w, conv_b):
    n, d = x.shape
    out_shape, out_specs = [], []
    for (_, _, cols, tr), dt in zip(segs, out_dtypes):
        if tr:
            out_shape.append(jax.ShapeDtypeStruct((cols, n), dt))
            out_specs.append(pl.BlockSpec((cols, tm), lambda i: (0, i)))
        else:
            out_shape.append(jax.ShapeDtypeStruct((n, cols), dt))
            out_specs.append(pl.BlockSpec((tm, cols), lambda i: (i, 0)))
    hb = tm // HALO
    return pl.pallas_call(
        functools.partial(_inproj_body, n_arr=len(arrays), segs=tuple(segs), col_chunk=512, conv_seg=conv_seg,
                          tiles_per_seq=seq_len // tm),
        out_shape=out_shape,
        grid=(n // tm,),
        in_specs=[pl.BlockSpec((tm, d), lambda i: (i, 0)),
                  pl.BlockSpec((HALO, d), lambda i: (jnp.maximum(i * hb - 1, 0), 0)),
                  pl.BlockSpec((HALO, d), lambda i: (jnp.minimum((i + 1) * hb, n // HALO - 1), 0)),
                  _layer_spec(norm_g, l), _mod_spec(l, 0, row_of_tile), _mod_spec(l, 1, row_of_tile),
                  _layer_spec(conv_w, l), _layer_spec(conv_b, l)]
                 + [_layer_spec(w, l) for w in arrays],
        out_specs=out_specs,
        compiler_params=_params(1),
        name="inproj",
    )(x, x, x, norm_g, mods, mods, conv_w, conv_b, *arrays)


def _split3(v):
    hi = v.astype(BF16)
    r1 = v - hi.astype(F32)
    mid = r1.astype(BF16)
    lo = (r1 - mid.astype(F32)).astype(BF16)
    return hi, mid, lo


def _ssd_body(xm_ref, dt_ref, z_ref, alog_ref, dtb_ref, dsk_ref,
              ng_ref, init_ref, y_ref, st_ref, yf_all, pt_ref, ec_ref, *, n_steps, ts):
    s = pl.program_id(1)
    fwd = s < n_steps
    c = jnp.where(fwd, s, 2 * n_steps - 1 - s)
    base = pl.multiple_of(c * ts, ts)
    n_sub = ts // CHUNK
    nh = SSD_HEADS
    inner = SSD_INNER
    half_grp = SSD_GROUPS * SSD_STATE
    pairs_per_group = SSD_HEADS // SSD_GROUPS // 2

    @pl.when(jnp.logical_or(s == 0, s == n_steps))
    def _():
        st_ref[...] = init_ref[...]

    @pl.when(fwd)
    def _():
        yf_all[pl.ds(base, ts), :] = xm_ref[0, :, :inner].astype(F32) * dsk_ref[0]

    li = lax.broadcasted_iota(jnp.int32, (CHUNK, CHUNK), 0)
    si = lax.broadcasted_iota(jnp.int32, (CHUNK, CHUNK), 1)
    tri = jnp.where(fwd, li - si, si - li) >= 0
    first_half = si < SSD_HEAD_DIM
    first_half_row = first_half[0:1]

    dtr = dt_ref[0]
    dtr_t = jnp.concatenate([dtr[j * CHUNK:(j + 1) * CHUNK].T[:nh] for j in range(n_sub)], axis=0)
    dtr_t = dtr_t + jnp.tile(dtb_ref[0, 0], (n_sub, 1))
    dt_t = jnp.maximum(dtr_t, 0.0) + jnp.log(1.0 + jnp.exp(-jnp.abs(dtr_t)))
    a2_t = dt_t * jnp.tile(-jnp.exp(alog_ref[0, 0]) * LOG2E, (n_sub, 1))
    order = jnp.where(tri, 1.0, 0.0).astype(BF16)
    pieces = _split3(a2_t)
    e2_t = sum(_dot_nt(p, order) for p in pieces)
    tot2 = jnp.sum(a2_t, axis=1, keepdims=True)
    er2_t = e2_t - jnp.log(dt_t) * LOG2E
    pt_ref[0] = er2_t
    pt_ref[1] = jnp.exp2(tot2 - er2_t)
    pt_ref[2] = jnp.broadcast_to(jnp.exp2(tot2), e2_t.shape)
    zpad = jnp.zeros((CHUNK - nh, CHUNK), BF16)
    for j in range(n_sub):
        ec_ref[j * CHUNK:(j + 1) * CHUNK, :] = sum(
            _dot_nt(order, jnp.concatenate([p[j * nh:(j + 1) * nh], zpad], axis=0)) for p in pieces)

    for j in range(n_sub):
        jj = jnp.where(fwd, j, n_sub - 1 - j)
        loc = pl.multiple_of(jj * CHUNK, CHUNK)
        off = pl.multiple_of(base + loc, CHUNK)
        hrow = pl.multiple_of(jj * nh, nh)
        xc = xm_ref[0, pl.ds(loc, CHUNK), :]
        er2 = pt_ref[0, pl.ds(hrow, nh), :]
        w_t = pt_ref[1, pl.ds(hrow, nh), :]
        etot = pt_ref[2, pl.ds(hrow, nh), :]
        e_col = ec_ref[pl.ds(loc, CHUNK), :]
        y_pairs = []
        for g in range(SSD_GROUPS):
            b_g = xc[:, inner + g * SSD_STATE: inner + (g + 1) * SSD_STATE]
            c_g = xc[:, inner + half_grp + g * SSD_STATE: inner + half_grp + (g + 1) * SSD_STATE]
            cb = _dot_nt(c_g, b_g).astype(BF16)
            b_t = b_g.astype(F32).T.astype(BF16)
            gcols = slice(g * inner // SSD_GROUPS, (g + 1) * inner // SSD_GROUPS)
            y_carry = _dot(c_g, st_ref[0, 0, :, gcols].astype(BF16))
            for pr in range(pairs_per_group):
                pair = g * pairs_per_group + pr
                cols = slice(pair * LANES, (pair + 1) * LANES)
                xp = xc[:, cols]
                zero = jnp.zeros_like(xp)
                x12 = jnp.concatenate([jnp.where(first_half, xp, zero), jnp.where(first_half, zero, xp)], axis=0)
                ms, bw, dec = [], [], []
                for h in (2 * pair, 2 * pair + 1):
                    ecol = jnp.broadcast_to(e_col[:, h:h + 1], (CHUNK, CHUNK))
                    ms.append(jnp.exp2(jnp.where(tri, ecol - er2[h:h + 1, :], -1e30)).astype(BF16) * cb)
                    dec.append(jnp.exp2(ecol))
                    w_rows = jnp.broadcast_to(w_t[h:h + 1, :], (2 * 8, CHUNK)).astype(BF16)
                    bw.append(b_t * jnp.tile(w_rows, (SSD_STATE // (2 * 8), 1)))
                both = _dot(jnp.concatenate([jnp.concatenate(ms, axis=1), jnp.concatenate(bw, axis=1)], axis=0), x12)
                y_pairs.append(both[:CHUNK] + jnp.where(first_half, dec[0], dec[1]) * y_carry[:, pr * LANES:(pr + 1) * LANES])
                erow = jnp.where(first_half_row, etot[2 * pair:2 * pair + 1, :], etot[2 * pair + 1:2 * pair + 2, :])
                st_ref[0, 0, :, cols] = st_ref[0, 0, :, cols] * erow + both[CHUNK:]
        yf_all[pl.ds(off, CHUNK), :] = yf_all[pl.ds(off, CHUNK), :] + jnp.concatenate(y_pairs, axis=1)

    @pl.when(jnp.logical_not(fwd))
    def _():
        y = yf_all[pl.ds(base, ts), :] * _silu(z_ref[0].astype(F32))
        gw = inner // SSD_GROUPS
        for g in range(SSD_GROUPS):
            yg = y[:, g * gw:(g + 1) * gw]
            ms = jnp.mean(yg * yg, axis=-1, keepdims=True)
            yn = (yg * lax.rsqrt(ms + EPS)) * ng_ref[0, :, g * gw:(g + 1) * gw]
            y_ref[0, :, g * gw:(g + 1) * gw] = yn.astype(y_ref.dtype)


def _ssd(xbc, dt, z, l, prm, init, ts):
    nb, t, _ = xbc.shape
    n_steps = t // ts

    def tok_out(s):
        return jnp.where(s < n_steps, n_steps - 1, 2 * n_steps - 1 - s)

    def tok_dt(s):
        return jnp.where(s < n_steps, s, 2 * n_steps - 1 - s)

    def phase(s):
        return jnp.where(s < n_steps, 0, 1)

    head_spec = pl.BlockSpec((1, 1, SSD_HEADS, LANES), lambda b, s: (l, phase(s), 0, 0))
    return pl.pallas_call(
        functools.partial(_ssd_body, n_steps=n_steps, ts=ts),
        out_shape=[jax.ShapeDtypeStruct((nb, t, SSD_INNER), BF16),
                   jax.ShapeDtypeStruct((nb, 2, SSD_STATE, SSD_INNER), F32)],
        grid=(nb, 2 * n_steps),
        in_specs=[
            pl.BlockSpec((1, ts, SSD_XBC), lambda b, s: (b, tok_dt(s), 0)),
            pl.BlockSpec((1, ts, LANES), lambda b, s: (b, tok_dt(s), phase(s))),
            pl.BlockSpec((1, ts, SSD_INNER), lambda b, s: (b, tok_out(s), 0)),
            head_spec,
            head_spec,
            _layer_spec(prm["d_full"], l),
            _layer_spec(prm["norm_g"], l),
            pl.BlockSpec((1, 1, SSD_STATE, SSD_INNER), lambda b, s: (b, phase(s), 0, 0)),
        ],
        out_specs=[
            pl.BlockSpec((1, ts, SSD_INNER), lambda b, s: (b, tok_out(s), 0)),
            pl.BlockSpec((1, 1, SSD_STATE, SSD_INNER), lambda b, s: (b, phase(s), 0, 0)),
        ],
        scratch_shapes=[pltpu.VMEM((t, SSD_INNER), F32),
                        pltpu.VMEM((3, (ts // CHUNK) * SSD_HEADS, CHUNK), F32), pltpu.VMEM((ts, LANES), F32)],
        compiler_params=_params(2),
        name="ssd_scan",
    )(xbc, dt, z, prm["a_log"], prm["dt_bias"], prm["d_full"], prm["norm_g"], init)


def _ssd_params(a_log, dt_bias, d_skip, norm_g):
    nl = a_log.shape[0]
    rep = lambda v: jnp.broadcast_to(v[:, :, :, None], (nl, 2, SSD_HEADS, LANES))
    return dict(
        a_log=rep(a_log), dt_bias=rep(dt_bias),
        d_full=jnp.repeat(d_skip, SSD_HEAD_DIM, axis=1).reshape(nl, 1, SSD_INNER),
        norm_g=norm_g.reshape(nl, 1, SSD_INNER))


def _cv_body(x_ref, w_ref, b_ref, lg_ref, lb_ref, o_ref, pad_ref, *, width, n_rows):
    zeros = jnp.zeros((CV_PAD, CONV_DIM), F32)
    pad_ref[0:CV_PAD, :] = zeros
    pad_ref[CV_PAD + width:, :] = zeros
    n_q = (CONV_WIDTH + 8) // 8

    def row_body(r, carry):
        xr = x_ref[r]
        u = xr[:, :CONV_DIM].astype(F32) * _sigmoid(xr[:, CONV_DIM:].astype(F32))
        pad_ref[CV_PAD:CV_PAD + width, :] = u
        cols = []
        for lt in range(CONV_DIM // LANES):
            ls = slice(lt * LANES, (lt + 1) * LANES)
            acc = jnp.broadcast_to(b_ref[0, :, ls], (width, LANES))
            for rr in range(8):
                shifted = pad_ref[rr:rr + width + 8 * (n_q - 1), ls]
                for q in range(n_q):
                    k = 8 * q + rr - (CV_PAD - CONV_WIDTH // 2)
                    if 0 <= k < CONV_WIDTH:
                        acc = acc + shifted[8 * q:8 * q + width] * w_ref[0, k:k + 1, ls]
            cols.append(acc)
        v = jnp.concatenate(cols, axis=1)
        mu = jnp.mean(v, axis=-1, keepdims=True)
        dv = v - mu
        var = jnp.mean(dv * dv, axis=-1, keepdims=True)
        yn = dv * lax.rsqrt(var + EPS) * lg_ref[0] + lb_ref[0]
        o_ref[r] = _silu(yn).astype(o_ref.dtype)
        return carry

    lax.fori_loop(0, n_rows, row_body, 0)


def _conv_branch(cv, width, l, dw_w, dw_b, ln_g, ln_b):
    n = cv.shape[0]
    rows = n // width
    n_rows = max(1, min(rows, 1024 // width))
    x3 = cv.reshape(rows, width, 2 * CONV_DIM)
    out = pl.pallas_call(
        functools.partial(_cv_body, width=width, n_rows=n_rows),
        out_shape=jax.ShapeDtypeStruct((rows, width, CONV_DIM), BF16),
        grid=(rows // n_rows,),
        in_specs=[pl.BlockSpec((n_rows, width, 2 * CONV_DIM), lambda i: (i, 0, 0)),
                  _layer_spec(dw_w, l), _layer_spec(dw_b, l), _layer_spec(ln_g, l), _layer_spec(ln_b, l)],
        out_specs=pl.BlockSpec((n_rows, width, CONV_DIM), lambda i: (i, 0, 0)),
        scratch_shapes=[pltpu.VMEM((width + 2 * CV_PAD, CONV_DIM), F32)],
        compiler_params=_params(1),
        name="conv_branch",
    )(x3, dw_w, dw_b, ln_g, ln_b)
    return out.reshape(n, CONV_DIM)


def _toeplitz_pairs(dw_w):
    half = CONV_WIDTH // 2
    s_idx, t_idx = np.meshgrid(np.arange(LANES), np.arange(LANES), indexing="ij")
    same_row = (s_idx // GRID_W) == (t_idx // GRID_W)
    sel = np.stack([(s_idx - t_idx + half == k) & same_row for k in range(CONV_WIDTH)]).astype(np.float32)
    taps = jnp.swapaxes(dw_w, 1, 2).reshape(-1, CONV_WIDTH).astype(BF16)
    return jnp.einsum("ck,kst->cst", taps, jnp.asarray(sel, BF16), preferred_element_type=BF16)


def _cvmx_body(a_ref, b_ref, t_ref, bias_ref, o_ref, *, n_ch):
    def ch_body(c, carry):
        u = a_ref[c].astype(F32) * _sigmoid(b_ref[c].astype(F32))
        o_ref[c] = (_dot(u.astype(BF16), t_ref[c]) + bias_ref[c]).astype(o_ref.dtype)
        return carry

    lax.fori_loop(0, n_ch, ch_body, 0, unroll=8)


def _conv_branch_mxu(cv_t, l, toe, bias):
    n = cv_t.shape[1]
    sr = n // LANES
    rb = min(256, sr)
    cbk = 64
    x3 = cv_t.reshape(2 * CONV_DIM, sr, LANES)
    nblk = CONV_DIM // cbk
    out = pl.pallas_call(
        functools.partial(_cvmx_body, n_ch=cbk),
        out_shape=jax.ShapeDtypeStruct((CONV_DIM, sr, LANES), BF16),
        grid=(nblk, sr // rb),
        in_specs=[pl.BlockSpec((cbk, rb, LANES), lambda c, i: (c, i, 0)),
                  pl.BlockSpec((cbk, rb, LANES), lambda c, i: (c + nblk, i, 0)),
                  pl.BlockSpec((cbk, LANES, LANES), lambda c, i: (l * nblk + c, 0, 0)),
                  pl.BlockSpec((cbk, 1, LANES), lambda c, i: (l * nblk + c, 0, 0))],
        out_specs=pl.BlockSpec((cbk, rb, LANES), lambda c, i: (c, i, 0)),
        compiler_params=_params(2),
        name="conv_branch_mxu",
    )(x3, x3, toe, bias)
    return out.reshape(CONV_DIM, n)


def _pool_body(x_ref, ic_ref, o_ref, *, n):
    max_lo = max(POOL_WINDOWS) // 2
    for k, w in enumerate(POOL_WINDOWS):
        ls = slice(k * POOL_PAD, (k + 1) * POOL_PAD)
        x = x_ref[0, :, :, ls].astype(F32)
        zeros = jnp.zeros((max_lo,) + x.shape[1:], F32)
        run = jnp.concatenate([zeros, x, zeros], axis=0)
        span = 1
        while span < w:
            run = run[:-span] + run[span:]
            span *= 2
        lo = w // 2
        total = run[max_lo - lo:max_lo - lo + n]
        o_ref[0, :, :, ls] = (total * ic_ref[:, :, ls] - x).astype(o_ref.dtype)


def _pool_branch(pool_in, n, cb):
    nb, _, c, _ = pool_in.shape
    pos = np.arange(n)
    inv = np.zeros((n, 1, POOL_PADDED), np.float32)
    for k, w in enumerate(POOL_WINDOWS):
        lo = w // 2
        cnt = np.clip(pos + (w - 1 - lo) + 1, 0, n) - np.clip(pos - lo, 0, n)
        inv[:, 0, k * POOL_PAD:(k + 1) * POOL_PAD] = (1.0 / cnt)[:, None]
    return pl.pallas_call(
        functools.partial(_pool_body, n=n),
        out_shape=jax.ShapeDtypeStruct(pool_in.shape, BF16),
        grid=(nb, c // cb),
        in_specs=[pl.BlockSpec((1, n, cb, POOL_PADDED), lambda b, j: (b, 0, j, 0)),
                  pl.BlockSpec((n, 1, POOL_PADDED), lambda b, j: (0, 0, 0), pipeline_mode=pl.Buffered(1))],
        out_specs=pl.BlockSpec((1, n, cb, POOL_PADDED), lambda b, j: (b, 0, j, 0)),
        compiler_params=_params(2),
        name="pool_branch",
    )(pool_in, jnp.asarray(inv))


def _pool_seq_body(x_ref, band_ref, ic_ref, o_ref):
    for k in range(POOL_GROUPS):
        ls = slice(k * POOL_PAD, (k + 1) * POOL_PAD)
        x = x_ref[:, ls]
        o_ref[:, ls] = (_dot(band_ref[k], x) * ic_ref[k] - x.astype(F32)).astype(o_ref.dtype)


def _pool_seq(pool_in, n):
    pos = np.arange(n)
    band = np.zeros((POOL_GROUPS, n, n), np.float32)
    inv = np.zeros((POOL_GROUPS, n, POOL_PAD), np.float32)
    for k, w in enumerate(POOL_WINDOWS):
        lo = w // 2
        inside = (pos[None, :] >= pos[:, None] - lo) & (pos[None, :] <= pos[:, None] + (w - 1 - lo))
        band[k] = inside
        inv[k] = (1.0 / inside.sum(axis=1))[:, None]
    whole = lambda shape: pl.BlockSpec(shape, lambda b: (0,) * len(shape), pipeline_mode=pl.Buffered(1))
    return pl.pallas_call(
        _pool_seq_body,
        out_shape=jax.ShapeDtypeStruct(pool_in.shape, BF16),
        grid=(pool_in.shape[0] // n,),
        in_specs=[pl.BlockSpec((n, POOL_PADDED), lambda b: (b, 0)), whole(band.shape), whole(inv.shape)],
        out_specs=pl.BlockSpec((n, POOL_PADDED), lambda b: (b, 0)),
        compiler_params=_params(1),
        name="pool_seq",
    )(pool_in, jnp.asarray(band, BF16), jnp.asarray(inv))


def _merge_body(x_ref, ys_ref, uc_ref, pp_ref, gt_ref, mg_ref, ps_ref, lg_ref, lb_ref, wss_ref, wcv_ref, wpl_ref,
                wo_ref, o_ref, *, cv_transposed):
    d = D_MODEL
    y_ssd = _dot(ys_ref[...], wss_ref[0])
    if cv_transposed:
        v = uc_ref[...].astype(F32)
        mu = jnp.mean(v, axis=0, keepdims=True)
        dv = v - mu
        var = jnp.mean(dv * dv, axis=0, keepdims=True)
        reps = v.shape[1] // LANES
        yn = dv * lax.rsqrt(var + EPS) * jnp.tile(lg_ref[0], (1, reps)) + jnp.tile(lb_ref[0], (1, reps))
        y_cv = lax.dot_general(_silu(yn).astype(BF16), wcv_ref[0], (((0,), (0,)), ((), ())),
                               preferred_element_type=F32)
    else:
        y_cv = _dot(uc_ref[...], wcv_ref[0])
    y_pool = jnp.concatenate(
        [_dot(pp_ref[:, k * POOL_PAD:(k + 1) * POOL_PAD], wpl_ref[0, k]) for k in range(POOL_GROUPS)], axis=1)
    y_pool = y_pool * ps_ref[0]
    merged2 = ((jnp.tanh(gt_ref[:, 0:d].astype(F32)) + 1.0) * y_ssd
               + (jnp.tanh(gt_ref[:, d:2 * d].astype(F32)) + 1.0) * y_cv
               + (jnp.tanh(gt_ref[:, 2 * d:3 * d].astype(F32)) + 1.0) * y_pool)
    o_ref[...] = x_ref[...] + (0.5 * mg_ref[0, 0]) * _dot(merged2.astype(BF16), wo_ref[0])


def _merge(x, ys, uc, pp, gates, l, mods, row_of_tile, prm, tm, cv_transposed):
    n, d = x.shape
    tok = lambda w: pl.BlockSpec((tm, w), lambda i: (i, 0))
    uc_spec = pl.BlockSpec((CONV_DIM, tm), lambda i: (0, i)) if cv_transposed else tok(CONV_DIM)
    names = ("pool_scale", "ln_g_rows", "ln_b_rows", "w_ssd", "w_cv", "w_pool", "w_out")
    return pl.pallas_call(
        functools.partial(_merge_body, cv_transposed=cv_transposed),
        out_shape=jax.ShapeDtypeStruct((n, d), F32),
        grid=(n // tm,),
        in_specs=[tok(d), tok(SSD_INNER), uc_spec, tok(POOL_PADDED), tok(3 * d), _mod_spec(l, 2, row_of_tile)]
                 + [_layer_spec(prm[k], l) for k in names],
        out_specs=tok(d),
        compiler_params=_params(1),
        name="merge",
    )(x, ys, uc, pp, gates, mods, *[prm[k] for k in names])


def _ffn_body(x_ref, g_ref, sh_ref, sc_ref, gt_ref, wg_ref, wu_ref, wd_ref, fg_ref, o_ref, *, final, hid_chunk):
    x = x_ref[...]
    hb = _norm_mod(x, g_ref[0], sh_ref[0, 0], sc_ref[0, 0]).astype(BF16)
    acc = jnp.zeros(x.shape, F32)
    for j in range(0, FFN_HIDDEN, hid_chunk):
        je = min(j + hid_chunk, FFN_HIDDEN)
        gate = _dot(hb, wg_ref[0, :, j:je])
        up = _dot(hb, wu_ref[0, :, j:je])
        acc = acc + _dot((_silu(gate) * up).astype(BF16), wd_ref[0, j:je, :])
    out = x + gt_ref[0, 0] * acc
    if final:
        ms = jnp.mean(out * out, axis=-1, keepdims=True)
        out = (out * lax.rsqrt(ms + EPS)) * fg_ref[...]
    o_ref[...] = out


def _ffn(x, l, norm_g, mods, row_of_tile, w_gate, w_up, w_down, final_g, final, tm):
    n, d = x.shape
    return pl.pallas_call(
        functools.partial(_ffn_body, final=final, hid_chunk=6 * MXU_DIM),
        out_shape=jax.ShapeDtypeStruct((n, d), F32),
        grid=(n // tm,),
        in_specs=[pl.BlockSpec((tm, d), lambda i: (i, 0)), _layer_spec(norm_g, l),
                  _mod_spec(l, 3, row_of_tile), _mod_spec(l, 4, row_of_tile), _mod_spec(l, 5, row_of_tile),
                  _layer_spec(w_gate, l), _layer_spec(w_up, l), _layer_spec(w_down, l),
                  pl.BlockSpec((1, d), lambda i: (0, 0), pipeline_mode=pl.Buffered(1))],
        out_specs=pl.BlockSpec((tm, d), lambda i: (i, 0)),
        compiler_params=_params(1),
        name="ffn",
    )(x, norm_g, mods, mods, mods, w_gate, w_up, w_down, final_g.reshape(1, d))


W_MAIN_COLS = dict(z=(0, SSD_INNER), xbc=(SSD_INNER, SSD_XBC), dt=(SSD_INNER + SSD_XBC, DT_PAD),
                   pool=(SSD_INNER + SSD_XBC + DT_PAD, POOL_PADDED),
                   gates=(SSD_INNER + SSD_XBC + DT_PAD + POOL_PADDED, 3 * D_MODEL))


def _split_w_body(w_ref, main_ref, cv_ref, cvt_ref):
    o = [int(v) for v in np.cumsum([0, SSD_INNER, SSD_XBC, 2 * SSD_HEADS, 2 * CONV_DIM, POOL_DIM, 3 * D_MODEL])]
    rb = w_ref.shape[1]
    col = lambda a, b: w_ref[0, :, a:b]
    zeros = lambda n: jnp.zeros((rb, n), F32)
    parts = [col(o[0], o[2])]
    for k in range(2):
        parts += [col(o[2] + k * SSD_HEADS, o[2] + (k + 1) * SSD_HEADS), zeros(LANES - SSD_HEADS)]
    for k in range(POOL_GROUPS):
        parts += [col(o[4] + k * POOL_GROUP_DIM, o[4] + (k + 1) * POOL_GROUP_DIM), zeros(POOL_PAD - POOL_GROUP_DIM)]
    parts.append(0.5 * col(o[5], o[6]))
    off = 0
    for p in parts:
        main_ref[0, :, off:off + p.shape[1]] = p.astype(BF16)
        off += p.shape[1]
    cv = col(o[3], o[4])
    cv_ref[0] = cv.astype(BF16)
    cvt_ref[0] = cv.T.astype(BF16)


def _split_w_in(w):
    nl, d, n_in = w.shape
    rb = 256
    n_main = sum(c for _, c in W_MAIN_COLS.values())
    return pl.pallas_call(
        _split_w_body,
        out_shape=[jax.ShapeDtypeStruct((nl, d, n_main), BF16), jax.ShapeDtypeStruct((nl, d, 2 * CONV_DIM), BF16),
                   jax.ShapeDtypeStruct((nl, 2 * CONV_DIM, d), BF16)],
        grid=(nl, d // rb),
        in_specs=[pl.BlockSpec((1, rb, n_in), lambda l, i: (l, i, 0))],
        out_specs=[pl.BlockSpec((1, rb, n_main), lambda l, i: (l, i, 0)),
                   pl.BlockSpec((1, rb, 2 * CONV_DIM), lambda l, i: (l, i, 0)),
                   pl.BlockSpec((1, 2 * CONV_DIM, rb), lambda l, i: (l, 0, i))],
        compiler_params=_params(2),
        name="split_w_in",
    )(w)


def kernel(x, c, ctx, c_ctx, w_ada, b_ada, norm1_g, norm2_g, w_in, ssd_conv_w, ssd_conv_b, ssd_a_log,
           ssd_dt_bias, ssd_d, ssd_norm_g, ssd_w_out, cv_dw_w, cv_dw_b, cv_ln_g, cv_ln_b, cv_w_out, pool_w,
           pool_scale, w_out, ffn_w_gate, ffn_w_up, ffn_w_down, final_g):
    nb, t, d = x.shape
    tc = ctx.shape[1]
    depth = w_in.shape[0]
    rows = t // GRID_W

    n_cond = -(-(nb + 1) // 8) * 8
    cc = jnp.concatenate([c, c_ctx[None, :], jnp.zeros((n_cond - nb - 1, d), F32)], axis=0)
    mods = _ada(cc, w_ada, b_ada)

    tm_lat = min(512, t)
    tm_ctx = min(256, tc)
    ts_lat = min(1024, t)
    ts_ctx = min(256, tc)
    cb_lat = min(16, GRID_W)
    lat_row = lambda i: i // (t // tm_lat)
    ctx_row = lambda i: nb

    w_main, w_cv, w_cv_t = _split_w_in(w_in)
    seg = lambda name, arr=0, tr=False: (arr,) + W_MAIN_COLS[name] + (tr,)
    w_fg, w_fu, w_fd = (v.astype(BF16) for v in (ffn_w_gate, ffn_w_up, ffn_w_down))
    row3 = lambda v: v.reshape(depth, 1, v.shape[-1])
    norm1, norm2 = row3(norm1_g), row3(norm2_g)
    ln_rows = lambda v: jnp.broadcast_to(v[:, :, None], (depth, CONV_DIM, LANES))
    merge_prm = dict(
        pool_scale=row3(pool_scale), ln_g_rows=ln_rows(cv_ln_g), ln_b_rows=ln_rows(cv_ln_b),
        w_ssd=ssd_w_out.astype(BF16), w_cv=cv_w_out.astype(BF16),
        w_pool=jnp.pad(pool_w, ((0, 0), (0, 0), (0, POOL_PAD - POOL_GROUP_DIM), (0, 0))).astype(BF16),
        w_out=w_out.astype(BF16))
    ssd_prm = _ssd_params(ssd_a_log, ssd_dt_bias, ssd_d, ssd_norm_g)
    conv5 = (jnp.pad(ssd_conv_w, ((0, 0), (0, 8 - SSD_CONV), (0, 0))), row3(ssd_conv_b))
    cv_vpu_prm = (jnp.pad(cv_dw_w, ((0, 0), (0, 32 - CONV_WIDTH), (0, 0))), row3(cv_dw_b), row3(cv_ln_g), row3(cv_ln_b))
    toe = _toeplitz_pairs(cv_dw_w)
    cv_bias = jnp.broadcast_to(cv_dw_b.reshape(-1)[:, None, None], (depth * CONV_DIM, 1, LANES))

    xl = x.reshape(nb * t, d)
    xc = ctx.reshape(nb * tc, d)
    zero_state = jnp.zeros((nb, 2, SSD_STATE, SSD_INNER), F32)

    for l in range(depth):
        last = l == depth - 1

        if last:
            xbc_c, dt_c = _inproj(xc, l, norm1, mods, ctx_row, [w_main], [seg("xbc"), seg("dt")], [BF16, F32],
                                  tm_ctx, tc, 0, *conv5)
            z_c = jnp.zeros((nb, tc, SSD_INNER), BF16)
        else:
            z_c, xbc_c, dt_c, cv_c, pool_c, gates_c = _inproj(
                xc, l, norm1, mods, ctx_row, [w_main, w_cv],
                [seg("z"), seg("xbc"), seg("dt"), (1, 0, 2 * CONV_DIM, False), seg("pool"), seg("gates")],
                [BF16, BF16, F32, BF16, BF16, BF16], tm_ctx, tc, 1, *conv5)
            z_c = z_c.reshape(nb, tc, SSD_INNER)
        y_c, ctx_state = _ssd(xbc_c.reshape(nb, tc, SSD_XBC), dt_c.reshape(nb, tc, DT_PAD), z_c, l, ssd_prm,
                              zero_state, ts_ctx)
        if not last:
            uc_c = _conv_branch(cv_c, tc, l, *cv_vpu_prm)
            pp_c = _pool_seq(pool_c, tc)
            xc = _merge(xc, y_c.reshape(nb * tc, SSD_INNER), uc_c, pp_c, gates_c, l, mods, ctx_row, merge_prm,
                        tm_ctx, False)
            xc = _ffn(xc, l, norm2, mods, ctx_row, w_fg, w_fu, w_fd, final_g, False, tm_ctx)

        z_l, xbc_l, dt_l, cv_l, pool_l, gates_l = _inproj(
            xl, l, norm1, mods, lat_row, [w_main, w_cv_t],
            [seg("z"), seg("xbc"), seg("dt"), (1, 0, 2 * CONV_DIM, True), seg("pool"), seg("gates")],
            [BF16, BF16, F32, BF16, BF16, BF16], tm_lat, t, 1, *conv5)
        y_l, _ = _ssd(xbc_l.reshape(nb, t, SSD_XBC), dt_l.reshape(nb, t, DT_PAD), z_l.reshape(nb, t, SSD_INNER),
                      l, ssd_prm, ctx_state, ts_lat)
        v_l = _conv_branch_mxu(cv_l, l, toe, cv_bias)
        pp_l = _pool_branch(pool_l.reshape(nb, rows, GRID_W, POOL_PADDED), rows, cb_lat).reshape(-1, POOL_PADDED)
        xl = _merge(xl, y_l.reshape(nb * t, SSD_INNER), v_l, pp_l, gates_l, l, mods, lat_row, merge_prm,
                    tm_lat, True)
        xl = _ffn(xl, l, norm2, mods, lat_row, w_fg, w_fu, w_fd, final_g, last, tm_lat)

    return xl.reshape(nb, t, d)
```

```python
import functools

import numpy as np
import jax
import jax.numpy as jnp
from jax import lax
from jax.experimental import pallas as pl
from jax.experimental.pallas import tpu as pltpu

F32 = jnp.float32
BF16 = jnp.bfloat16

D_MODEL = 1024
GRID_W = 64
EPS = 1e-6
N_MOD = 6
SSD_HEADS = 16
SSD_HEAD_DIM = 64
SSD_INNER = SSD_HEADS * SSD_HEAD_DIM
SSD_GROUPS = 2
SSD_STATE = 128
SSD_CONV = 5
SSD_XBC = SSD_INNER + 2 * SSD_GROUPS * SSD_STATE
CONV_DIM = 768
CONV_WIDTH = 31
POOL_DIM = 768
POOL_WINDOWS = (2, 4, 8, 16)
POOL_GROUPS = len(POOL_WINDOWS)
POOL_GROUP_DIM = POOL_DIM // POOL_GROUPS
POOL_OUT_DIM = D_MODEL // POOL_GROUPS
FFN_HIDDEN = 2816

LANES = 128
MXU_DIM = 256
CHUNK = 128
LOG2E = 1.4426950408889634
POOL_PAD = 256
POOL_PADDED = POOL_GROUPS * POOL_PAD
DT_PAD = 2 * LANES
HALO = 16
CV_PAD = 16
VMEM_LIMIT = 56 * 1024 * 1024


def _layer_spec(arr, l):
    shape = arr.shape[1:]
    nd = len(shape)
    return pl.BlockSpec((1,) + shape, lambda *_: (l,) + (0,) * nd, pipeline_mode=pl.Buffered(1))


def _mod_spec(l, k, row_of_tile):
    return pl.BlockSpec((1, 1, 1, D_MODEL), lambda i: (l, row_of_tile(i), 0, k))


def _params(n_axes):
    return pltpu.CompilerParams(dimension_semantics=("arbitrary",) * n_axes, vmem_limit_bytes=VMEM_LIMIT)


def _sigmoid(v):
    return 0.5 * jnp.tanh(0.5 * v) + 0.5


def _silu(v):
    return v * _sigmoid(v)


def _dot(a, b):
    return jnp.dot(a, b, preferred_element_type=F32)


def _dot_nt(a, b):
    return lax.dot_general(a, b, (((1,), (1,)), ((), ())), preferred_element_type=F32)


def _ada_body(c_ref, w_ref, b_ref, o_ref):
    s = _silu(c_ref[...])
    o_ref[0] = jnp.dot(s, w_ref[0], preferred_element_type=F32,
                       precision=lax.Precision.HIGHEST) + b_ref[0]


def _ada(cc, w_ada, b_ada):
    nl, d, nm = w_ada.shape
    r = cc.shape[0]
    tn = 1024
    return pl.pallas_call(
        _ada_body,
        out_shape=jax.ShapeDtypeStruct((nl, r, nm), F32),
        grid=(nl, nm // tn),
        in_specs=[pl.BlockSpec((r, d), lambda l, j: (0, 0)),
                  pl.BlockSpec((1, d, tn), lambda l, j: (l, 0, j)),
                  pl.BlockSpec((1, 1, tn), lambda l, j: (l, 0, j))],
        out_specs=pl.BlockSpec((1, r, tn), lambda l, j: (l, 0, j)),
        compiler_params=_params(2),
        name="adaln",
    )(cc, w_ada, b_ada.reshape(nl, 1, nm)).reshape(nl, r, 1, nm)


def _norm_mod(x, g, shift, scale):
    ms = jnp.mean(x * x, axis=-1, keepdims=True)
    h = (x * lax.rsqrt(ms + EPS)) * g
    return h * (1.0 + scale) + shift


def _inproj_body(*refs, n_arr, segs, col_chunk, conv_seg, tiles_per_seq):
    x_ref, xp_ref, xn_ref, g_ref, sh_ref, sc_ref, cw_ref, cb_ref = refs[:8]
    w_refs = refs[8:8 + n_arr]
    o_refs = refs[8 + n_arr:]
    tm = x_ref.shape[0]
    norm = lambda v: _norm_mod(v, g_ref[0], sh_ref[0, 0], sc_ref[0, 0]).astype(BF16)
    hb = norm(x_ref[...])
    pos = pl.program_id(0) % tiles_per_seq
    h_prev = jnp.where(pos > 0, norm(xp_ref[...]), jnp.zeros((HALO, x_ref.shape[1]), BF16))
    h_next = jnp.where(pos < tiles_per_seq - 1, norm(xn_ref[...]), jnp.zeros((HALO, x_ref.shape[1]), BF16))
    hb_ext = jnp.concatenate([h_prev, hb, h_next], axis=0)
    for seg, ((arr, off, n, tr), o_ref) in enumerate(zip(segs, o_refs)):
        w_ref = w_refs[arr]
        step = min(col_chunk, n)
        for j in range(0, n, step):
            if tr:
                o_ref[j:j + step, :] = _dot_nt(w_ref[0, off + j:off + j + step, :], hb).astype(o_ref.dtype)
            elif seg == conv_seg:
                r = _dot(hb_ext, w_ref[0, :, off + j:off + j + step])
                rows = tm + 2 * HALO
                acc = jnp.broadcast_to(cb_ref[0, :, j:j + step], (tm, step))
                for k in range(SSD_CONV):
                    sh = (SSD_CONV // 2 - k) % rows
                    rk = r if sh == 0 else pltpu.roll(r, sh, axis=0)
                    acc = acc + rk[HALO:HALO + tm] * cw_ref[0, k:k + 1, j:j + step]
                hv = 0.5 * acc
                o_ref[:, j:j + step] = (hv + hv * jnp.tanh(hv)).astype(o_ref.dtype)
            else:
                o_ref[:, j:j + step] = _dot(hb, w_ref[0, :, off + j:off + j + step]).astype(o_ref.dtype)


def _inproj(x, l, norm_g, mods, row_of_tile, arrays, segs, out_dtypes, tm, seq_len, conv_seg, conv_w, conv_b):
    n, d = x.shape
    out_shape, out_specs = [], []
    for (_, _, cols, tr), dt in zip(segs, out_dtypes):
        if tr:
            out_shape.append(jax.ShapeDtypeStruct((cols, n), dt))
            out_specs.append(pl.BlockSpec((cols, tm), lambda i: (0, i)))
        else:
            out_shape.append(jax.ShapeDtypeStruct((n, cols), dt))
            out_specs.append(pl.BlockSpec((tm, cols), lambda i: (i, 0)))
    hb = tm // HALO
    return pl.pallas_call(
        functools.partial(_inproj_body, n_arr=len(arrays), segs=tuple(segs), col_chunk=512, conv_seg=conv_seg,
                          tiles_per_seq=seq_len // tm),
        out_shape=out_shape,
        grid=(n // tm,),
        in_specs=[pl.BlockSpec((tm, d), lambda i: (i, 0)),
                  pl.BlockSpec((HALO, d), lambda i: (jnp.maximum(i * hb - 1, 0), 0)),
                  pl.BlockSpec((HALO, d), lambda i: (jnp.minimum((i + 1) * hb, n // HALO - 1), 0)),
                  _layer_spec(norm_g, l), _mod_spec(l, 0, row_of_tile), _mod_spec(l, 1, row_of_tile),
                  _layer_spec(conv_w, l), _layer_spec(conv_b, l)]
                 + [_layer_spec(w, l) for w in arrays],
        out_specs=out_specs,
        compiler_params=_params(1),
        name="inproj",
    )(x, x, x, norm_g, mods, mods, conv_w, conv_b, *arrays)


def _split3(v):
    hi = v.astype(BF16)
    r1 = v - hi.astype(F32)
    mid = r1.astype(BF16)
    lo = (r1 - mid.astype(F32)).astype(BF16)
    return hi, mid, lo


def _ssd_body(xm_ref, dt_ref, z_ref, alog_ref, dtb_ref, dsk_ref,
              ng_ref, init_ref, y_ref, st_ref, yf_all, pt_ref, ec_ref, *, n_steps, ts):
    s = pl.program_id(1)
    fwd = s < n_steps
    c = jnp.where(fwd, s, 2 * n_steps - 1 - s)
    base = pl.multiple_of(c * ts, ts)
    n_sub = ts // CHUNK
    nh = SSD_HEADS
    inner = SSD_INNER
    half_grp = SSD_GROUPS * SSD_STATE
    pairs_per_group = SSD_HEADS // SSD_GROUPS // 2

    @pl.when(jnp.logical_or(s == 0, s == n_steps))
    def _():
        st_ref[...] = init_ref[...]

    @pl.when(fwd)
    def _():
        yf_all[pl.ds(base, ts), :] = xm_ref[0, :, :inner].astype(F32) * dsk_ref[0]

    li = lax.broadcasted_iota(jnp.int32, (CHUNK, CHUNK), 0)
    si = lax.broadcasted_iota(jnp.int32, (CHUNK, CHUNK), 1)
    tri = jnp.where(fwd, li - si, si - li) >= 0
    first_half = si < SSD_HEAD_DIM
    first_half_row = first_half[0:1]

    dtr = dt_ref[0]
    dtr_t = jnp.concatenate([dtr[j * CHUNK:(j + 1) * CHUNK].T[:nh] for j in range(n_sub)], axis=0)
    dtr_t = dtr_t + jnp.tile(dtb_ref[0, 0], (n_sub, 1))
    dt_t = jnp.maximum(dtr_t, 0.0) + jnp.log(1.0 + jnp.exp(-jnp.abs(dtr_t)))
    a2_t = dt_t * jnp.tile(-jnp.exp(alog_ref[0, 0]) * LOG2E, (n_sub, 1))
    order = jnp.where(tri, 1.0, 0.0).astype(BF16)
    pieces = _split3(a2_t)
    e2_t = sum(_dot_nt(p, order) for p in pieces)
    tot2 = jnp.sum(a2_t, axis=1, keepdims=True)
    er2_t = e2_t - jnp.log(dt_t) * LOG2E
    pt_ref[0] = er2_t
    pt_ref[1] = jnp.exp2(tot2 - er2_t)
    pt_ref[2] = jnp.broadcast_to(jnp.exp2(tot2), e2_t.shape)
    zpad = jnp.zeros((CHUNK - nh, CHUNK), BF16)
    for j in range(n_sub):
        ec_ref[j * CHUNK:(j + 1) * CHUNK, :] = sum(
            _dot_nt(order, jnp.concatenate([p[j * nh:(j + 1) * nh], zpad], axis=0)) for p in pieces)

    for j in range(n_sub):
        jj = jnp.where(fwd, j, n_sub - 1 - j)
        loc = pl.multiple_of(jj * CHUNK, CHUNK)
        off = pl.multiple_of(base + loc, CHUNK)
        hrow = pl.multiple_of(jj * nh, nh)
        xc = xm_ref[0, pl.ds(loc, CHUNK), :]
        er2 = pt_ref[0, pl.ds(hrow, nh), :]
        w_t = pt_ref[1, pl.ds(hrow, nh), :]
        etot = pt_ref[2, pl.ds(hrow, nh), :]
        e_col = ec_ref[pl.ds(loc, CHUNK), :]
        y_pairs = []
        for g in range(SSD_GROUPS):
            b_g = xc[:, inner + g * SSD_STATE: inner + (g + 1) * SSD_STATE]
            c_g = xc[:, inner + half_grp + g * SSD_STATE: inner + half_grp + (g + 1) * SSD_STATE]
            cb = _dot_nt(c_g, b_g).astype(BF16)
            b_t = b_g.astype(F32).T.astype(BF16)
            gcols = slice(g * inner // SSD_GROUPS, (g + 1) * inner // SSD_GROUPS)
            y_carry = _dot(c_g, st_ref[0, 0, :, gcols].astype(BF16))
            for pr in range(pairs_per_group):
                pair = g * pairs_per_group + pr
                cols = slice(pair * LANES, (pair + 1) * LANES)
                xp = xc[:, cols]
                zero = jnp.zeros_like(xp)
                x12 = jnp.concatenate([jnp.where(first_half, xp, zero), jnp.where(first_half, zero, xp)], axis=0)
                ms, bw, dec = [], [], []
                for h in (2 * pair, 2 * pair + 1):
                    ecol = jnp.broadcast_to(e_col[:, h:h + 1], (CHUNK, CHUNK))
                    ms.append(jnp.exp2(jnp.where(tri, ecol - er2[h:h + 1, :], -1e30)).astype(BF16) * cb)
                    dec.append(jnp.exp2(ecol))
                    w_rows = jnp.broadcast_to(w_t[h:h + 1, :], (2 * 8, CHUNK)).astype(BF16)
                    bw.append(b_t * jnp.tile(w_rows, (SSD_STATE // (2 * 8), 1)))
                both = _dot(jnp.concatenate([jnp.concatenate(ms, axis=1), jnp.concatenate(bw, axis=1)], axis=0), x12)
                y_pairs.append(both[:CHUNK] + jnp.where(first_half, dec[0], dec[1]) * y_carry[:, pr * LANES:(pr + 1) * LANES])
                erow = jnp.where(first_half_row, etot[2 * pair:2 * pair + 1, :], etot[2 * pair + 1:2 * pair + 2, :])
                st_ref[0, 0, :, cols] = st_ref[0, 0, :, cols] * erow + both[CHUNK:]
        yf_all[pl.ds(off, CHUNK), :] = yf_all[pl.ds(off, CHUNK), :] + jnp.concatenate(y_pairs, axis=1)

    @pl.when(jnp.logical_not(fwd))
    def _():
        y = yf_all[pl.ds(base, ts), :] * _silu(z_ref[0].astype(F32))
        gw = inner // SSD_GROUPS
        for g in range(SSD_GROUPS):
            yg = y[:, g * gw:(g + 1) * gw]
            ms = jnp.mean(yg * yg, axis=-1, keepdims=True)
            yn = (yg * lax.rsqrt(ms + EPS)) * ng_ref[0, :, g * gw:(g + 1) * gw]
            y_ref[0, :, g * gw:(g + 1) * gw] = yn.astype(y_ref.dtype)


def _ssd(xbc, dt, z, l, prm, init, ts):
    nb, t, _ = xbc.shape
    n_steps = t // ts

    def tok_out(s):
        return jnp.where(s < n_steps, n_steps - 1, 2 * n_steps - 1 - s)

    def tok_dt(s):
        return jnp.where(s < n_steps, s, 2 * n_steps - 1 - s)

    def phase(s):
        return jnp.where(s < n_steps, 0, 1)

    head_spec = pl.BlockSpec((1, 1, SSD_HEADS, LANES), lambda b, s: (l, phase(s), 0, 0))
    return pl.pallas_call(
        functools.partial(_ssd_body, n_steps=n_steps, ts=ts),
        out_shape=[jax.ShapeDtypeStruct((nb, t, SSD_INNER), BF16),
                   jax.ShapeDtypeStruct((nb, 2, SSD_STATE, SSD_INNER), F32)],
        grid=(nb, 2 * n_steps),
        in_specs=[
            pl.BlockSpec((1, ts, SSD_XBC), lambda b, s: (b, tok_dt(s), 0)),
            pl.BlockSpec((1, ts, LANES), lambda b, s: (b, tok_dt(s), phase(s))),
            pl.BlockSpec((1, ts, SSD_INNER), lambda b, s: (b, tok_out(s), 0)),
            head_spec,
            head_spec,
            _layer_spec(prm["d_full"], l),
            _layer_spec(prm["norm_g"], l),
            pl.BlockSpec((1, 1, SSD_STATE, SSD_INNER), lambda b, s: (b, phase(s), 0, 0)),
        ],
        out_specs=[
            pl.BlockSpec((1, ts, SSD_INNER), lambda b, s: (b, tok_out(s), 0)),
            pl.BlockSpec((1, 1, SSD_STATE, SSD_INNER), lambda b, s: (b, phase(s), 0, 0)),
        ],
        scratch_shapes=[pltpu.VMEM((t, SSD_INNER), F32),
                        pltpu.VMEM((3, (ts // CHUNK) * SSD_HEADS, CHUNK), F32), pltpu.VMEM((ts, LANES), F32)],
        compiler_params=_params(2),
        name="ssd_scan",
    )(xbc, dt, z, prm["a_log"], prm["dt_bias"], prm["d_full"], prm["norm_g"], init)


def _ssd_params(a_log, dt_bias, d_skip, norm_g):
    nl = a_log.shape[0]
    rep = lambda v: jnp.broadcast_to(v[:, :, :, None], (nl, 2, SSD_HEADS, LANES))
    return dict(
        a_log=rep(a_log), dt_bias=rep(dt_bias),
        d_full=jnp.repeat(d_skip, SSD_HEAD_DIM, axis=1).reshape(nl, 1, SSD_INNER),
        norm_g=norm_g.reshape(nl, 1, SSD_INNER))


def _cv_body(x_ref, w_ref, b_ref, lg_ref, lb_ref, o_ref, pad_ref, *, width, n_rows):
    zeros = jnp.zeros((CV_PAD, CONV_DIM), F32)
    pad_ref[0:CV_PAD, :] = zeros
    pad_ref[CV_PAD + width:, :] = zeros
    n_q = (CONV_WIDTH + 8) // 8

    def row_body(r, carry):
        xr = x_ref[r]
        u = xr[:, :CONV_DIM].astype(F32) * _sigmoid(xr[:, CONV_DIM:].astype(F32))
        pad_ref[CV_PAD:CV_PAD + width, :] = u
        cols = []
        for lt in range(CONV_DIM // LANES):
            ls = slice(lt * LANES, (lt + 1) * LANES)
            acc = jnp.broadcast_to(b_ref[0, :, ls], (width, LANES))
            for rr in range(8):
                shifted = pad_ref[rr:rr + width + 8 * (n_q - 1), ls]
                for q in range(n_q):
                    k = 8 * q + rr - (CV_PAD - CONV_WIDTH // 2)
                    if 0 <= k < CONV_WIDTH:
                        acc = acc + shifted[8 * q:8 * q + width] * w_ref[0, k:k + 1, ls]
            cols.append(acc)
        v = jnp.concatenate(cols, axis=1)
        mu = jnp.mean(v, axis=-1, keepdims=True)
        dv = v - mu
        var = jnp.mean(dv * dv, axis=-1, keepdims=True)
        yn = dv * lax.rsqrt(var + EPS) * lg_ref[0] + lb_ref[0]
        o_ref[r] = _silu(yn).astype(o_ref.dtype)
        return carry

    lax.fori_loop(0, n_rows, row_body, 0)


def _conv_branch(cv, width, l, dw_w, dw_b, ln_g, ln_b):
    n = cv.shape[0]
    rows = n // width
    n_rows = max(1, min(rows, 1024 // width))
    x3 = cv.reshape(rows, width, 2 * CONV_DIM)
    out = pl.pallas_call(
        functools.partial(_cv_body, width=width, n_rows=n_rows),
        out_shape=jax.ShapeDtypeStruct((rows, width, CONV_DIM), BF16),
        grid=(rows // n_rows,),
        in_specs=[pl.BlockSpec((n_rows, width, 2 * CONV_DIM), lambda i: (i, 0, 0)),
                  _layer_spec(dw_w, l), _layer_spec(dw_b, l), _layer_spec(ln_g, l), _layer_spec(ln_b, l)],
        out_specs=pl.BlockSpec((n_rows, width, CONV_DIM), lambda i: (i, 0, 0)),
        scratch_shapes=[pltpu.VMEM((width + 2 * CV_PAD, CONV_DIM), F32)],
        compiler_params=_params(1),
        name="conv_branch",
    )(x3, dw_w, dw_b, ln_g, ln_b)
    return out.reshape(n, CONV_DIM)


def _toeplitz_pairs(dw_w):
    half = CONV_WIDTH // 2
    s_idx, t_idx = np.meshgrid(np.arange(LANES), np.arange(LANES), indexing="ij")
    same_row = (s_idx // GRID_W) == (t_idx // GRID_W)
    sel = np.stack([(s_idx - t_idx + half == k) & same_row for k in range(CONV_WIDTH)]).astype(np.float32)
    taps = jnp.swapaxes(dw_w, 1, 2).reshape(-1, CONV_WIDTH).astype(BF16)
    return jnp.einsum("ck,kst->cst", taps, jnp.asarray(sel, BF16), preferred_element_type=BF16)


def _cvmx_body(a_ref, b_ref, t_ref, bias_ref, o_ref, *, n_ch):
    def ch_body(c, carry):
        u = a_ref[c].astype(F32) * _sigmoid(b_ref[c].astype(F32))
        o_ref[c] = (_dot(u.astype(BF16), t_ref[c]) + bias_ref[c]).astype(o_ref.dtype)
        return carry

    lax.fori_loop(0, n_ch, ch_body, 0, unroll=8)


def _conv_branch_mxu(cv_t, l, toe, bias):
    n = cv_t.shape[1]
    sr = n // LANES
    rb = min(256, sr)
    cbk = 64
    x3 = cv_t.reshape(2 * CONV_DIM, sr, LANES)
    nblk = CONV_DIM // cbk
    out = pl.pallas_call(
        functools.partial(_cvmx_body, n_ch=cbk),
        out_shape=jax.ShapeDtypeStruct((CONV_DIM, sr, LANES), BF16),
        grid=(nblk, sr // rb),
        in_specs=[pl.BlockSpec((cbk, rb, LANES), lambda c, i: (c, i, 0)),
                  pl.BlockSpec((cbk, rb, LANES), lambda c, i: (c + nblk, i, 0)),
                  pl.BlockSpec((cbk, LANES, LANES), lambda c, i: (l * nblk + c, 0, 0)),
                  pl.BlockSpec((cbk, 1, LANES), lambda c, i: (l * nblk + c, 0, 0))],
        out_specs=pl.BlockSpec((cbk, rb, LANES), lambda c, i: (c, i, 0)),
        compiler_params=_params(2),
        name="conv_branch_mxu",
    )(x3, x3, toe, bias)
    return out.reshape(CONV_DIM, n)


def _pool_body(x_ref, ic_ref, o_ref, *, n):
    max_lo = max(POOL_WINDOWS) // 2
    for k, w in enumerate(POOL_WINDOWS):
        ls = slice(k * POOL_PAD, (k + 1) * POOL_PAD)
        x = x_ref[0, :, :, ls].astype(F32)
        zeros = jnp.zeros((max_lo,) + x.shape[1:], F32)
        run = jnp.concatenate([zeros, x, zeros], axis=0)
        span = 1
        while span < w:
            run = run[:-span] + run[span:]
            span *= 2
        lo = w // 2
        total = run[max_lo - lo:max_lo - lo + n]
        o_ref[0, :, :, ls] = (total * ic_ref[:, :, ls] - x).astype(o_ref.dtype)


def _pool_branch(pool_in, n, cb):
    nb, _, c, _ = pool_in.shape
    pos = np.arange(n)
    inv = np.zeros((n, 1, POOL_PADDED), np.float32)
    for k, w in enumerate(POOL_WINDOWS):
        lo = w // 2
        cnt = np.clip(pos + (w - 1 - lo) + 1, 0, n) - np.clip(pos - lo, 0, n)
        inv[:, 0, k * POOL_PAD:(k + 1) * POOL_PAD] = (1.0 / cnt)[:, None]
    return pl.pallas_call(
        functools.partial(_pool_body, n=n),
        out_shape=jax.ShapeDtypeStruct(pool_in.shape, BF16),
        grid=(nb, c // cb),
        in_specs=[pl.BlockSpec((1, n, cb, POOL_PADDED), lambda b, j: (b, 0, j, 0)),
                  pl.BlockSpec((n, 1, POOL_PADDED), lambda b, j: (0, 0, 0), pipeline_mode=pl.Buffered(1))],
        out_specs=pl.BlockSpec((1, n, cb, POOL_PADDED), lambda b, j: (b, 0, j, 0)),
        compiler_params=_params(2),
        name="pool_branch",
    )(pool_in, jnp.asarray(inv))


def _pool_seq_body(x_ref, band_ref, ic_ref, o_ref):
    for k in range(POOL_GROUPS):
        ls = slice(k * POOL_PAD, (k + 1) * POOL_PAD)
        x = x_ref[:, ls]
        o_ref[:, ls] = (_dot(band_ref[k], x) * ic_ref[k] - x.astype(F32)).astype(o_ref.dtype)


def _pool_seq(pool_in, n):
    pos = np.arange(n)
    band = np.zeros((POOL_GROUPS, n, n), np.float32)
    inv = np.zeros((POOL_GROUPS, n, POOL_PAD), np.float32)
    for k, w in enumerate(POOL_WINDOWS):
        lo = w // 2
        inside = (pos[None, :] >= pos[:, None] - lo) & (pos[None, :] <= pos[:, None] + (w - 1 - lo))
        band[k] = inside
        inv[k] = (1.0 / inside.sum(axis=1))[:, None]
    whole = lambda shape: pl.BlockSpec(shape, lambda b: (0,) * len(shape), pipeline_mode=pl.Buffered(1))
    return pl.pallas_call(
        _pool_seq_body,
        out_shape=jax.ShapeDtypeStruct(pool_in.shape, BF16),
        grid=(pool_in.shape[0] // n,),
        in_specs=[pl.BlockSpec((n, POOL_PADDED), lambda b: (b, 0)), whole(band.shape), whole(inv.shape)],
        out_specs=pl.BlockSpec((n, POOL_PADDED), lambda b: (b, 0)),
        compiler_params=_params(1),
        name="pool_seq",
    )(pool_in, jnp.asarray(band, BF16), jnp.asarray(inv))


def _merge_body(x_ref, ys_ref, uc_ref, pp_ref, gt_ref, mg_ref, ps_ref, lg_ref, lb_ref, wss_ref, wcv_ref, wpl_ref,
                wo_ref, o_ref, *, cv_transposed):
    d = D_MODEL
    y_ssd = _dot(ys_ref[...], wss_ref[0])
    if cv_transposed:
        v = uc_ref[...].astype(F32)
        mu = jnp.mean(v, axis=0, keepdims=True)
        dv = v - mu
        var = jnp.mean(dv * dv, axis=0, keepdims=True)
        reps = v.shape[1] // LANES
        yn = dv * lax.rsqrt(var + EPS) * jnp.tile(lg_ref[0], (1, reps)) + jnp.tile(lb_ref[0], (1, reps))
        y_cv = lax.dot_general(_silu(yn).astype(BF16), wcv_ref[0], (((0,), (0,)), ((), ())),
                               preferred_element_type=F32)
    else:
        y_cv = _dot(uc_ref[...], wcv_ref[0])
    y_pool = jnp.concatenate(
        [_dot(pp_ref[:, k * POOL_PAD:(k + 1) * POOL_PAD], wpl_ref[0, k]) for k in range(POOL_GROUPS)], axis=1)
    y_pool = y_pool * ps_ref[0]
    merged2 = ((jnp.tanh(gt_ref[:, 0:d].astype(F32)) + 1.0) * y_ssd
               + (jnp.tanh(gt_ref[:, d:2 * d].astype(F32)) + 1.0) * y_cv
               + (jnp.tanh(gt_ref[:, 2 * d:3 * d].astype(F32)) + 1.0) * y_pool)
    o_ref[...] = x_ref[...] + (0.5 * mg_ref[0, 0]) * _dot(merged2.astype(BF16), wo_ref[0])


def _merge(x, ys, uc, pp, gates, l, mods, row_of_tile, prm, tm, cv_transposed):
    n, d = x.shape
    tok = lambda w: pl.BlockSpec((tm, w), lambda i: (i, 0))
    uc_spec = pl.BlockSpec((CONV_DIM, tm), lambda i: (0, i)) if cv_transposed else tok(CONV_DIM)
    names = ("pool_scale", "ln_g_rows", "ln_b_rows", "w_ssd", "w_cv", "w_pool", "w_out")
    return pl.pallas_call(
        functools.partial(_merge_body, cv_transposed=cv_transposed),
        out_shape=jax.ShapeDtypeStruct((n, d), F32),
        grid=(n // tm,),
        in_specs=[tok(d), tok(SSD_INNER), uc_spec, tok(POOL_PADDED), tok(3 * d), _mod_spec(l, 2, row_of_tile)]
                 + [_layer_spec(prm[k], l) for k in names],
        out_specs=tok(d),
        compiler_params=_params(1),
        name="merge",
    )(x, ys, uc, pp, gates, mods, *[prm[k] for k in names])


def _ffn_body(x_ref, g_ref, sh_ref, sc_ref, gt_ref, wg_ref, wu_ref, wd_ref, fg_ref, o_ref, *, final, hid_chunk):
    x = x_ref[...]
    hb = _norm_mod(x, g_ref[0], sh_ref[0, 0], sc_ref[0, 0]).astype(BF16)
    acc = jnp.zeros(x.shape, F32)
    for j in range(0, FFN_HIDDEN, hid_chunk):
        je = min(j + hid_chunk, FFN_HIDDEN)
        gate = _dot(hb, wg_ref[0, :, j:je])
        up = _dot(hb, wu_ref[0, :, j:je])
        acc = acc + _dot((_silu(gate) * up).astype(BF16), wd_ref[0, j:je, :])
    out = x + gt_ref[0, 0] * acc
    if final:
        ms = jnp.mean(out * out, axis=-1, keepdims=True)
        out = (out * lax.rsqrt(ms + EPS)) * fg_ref[...]
    o_ref[...] = out


def _ffn(x, l, norm_g, mods, row_of_tile, w_gate, w_up, w_down, final_g, final, tm):
    n, d = x.shape
    return pl.pallas_call(
        functools.partial(_ffn_body, final=final, hid_chunk=6 * MXU_DIM),
        out_shape=jax.ShapeDtypeStruct((n, d), F32),
        grid=(n // tm,),
        in_specs=[pl.BlockSpec((tm, d), lambda i: (i, 0)), _layer_spec(norm_g, l),
                  _mod_spec(l, 3, row_of_tile), _mod_spec(l, 4, row_of_tile), _mod_spec(l, 5, row_of_tile),
                  _layer_spec(w_gate, l), _layer_spec(w_up, l), _layer_spec(w_down, l),
                  pl.BlockSpec((1, d), lambda i: (0, 0), pipeline_mode=pl.Buffered(1))],
        out_specs=pl.BlockSpec((tm, d), lambda i: (i, 0)),
        compiler_params=_params(1),
        name="ffn",
    )(x, norm_g, mods, mods, mods, w_gate, w_up, w_down, final_g.reshape(1, d))


W_MAIN_COLS = dict(z=(0, SSD_INNER), xbc=(SSD_INNER, SSD_XBC), dt=(SSD_INNER + SSD_XBC, DT_PAD),
                   pool=(SSD_INNER + SSD_XBC + DT_PAD, POOL_PADDED),
                   gates=(SSD_INNER + SSD_XBC + DT_PAD + POOL_PADDED, 3 * D_MODEL))


def _split_w_body(w_ref, main_ref, cv_ref, cvt_ref):
    o = [int(v) for v in np.cumsum([0, SSD_INNER, SSD_XBC, 2 * SSD_HEADS, 2 * CONV_DIM, POOL_DIM, 3 * D_MODEL])]
    rb = w_ref.shape[2]
    rows = lambda a, b: w_ref[0, a:b, :]

    def padded(a, b, n):
        return jnp.concatenate([rows(a, b), jnp.zeros((n - (b - a), rb), F32)], axis=0)

    parts = [rows(o[0], o[2])]
    for k in range(2):
        parts.append(padded(o[2] + k * SSD_HEADS, o[2] + (k + 1) * SSD_HEADS, LANES))
    for k in range(POOL_GROUPS):
        parts.append(padded(o[4] + k * POOL_GROUP_DIM, o[4] + (k + 1) * POOL_GROUP_DIM, POOL_PAD))
    parts.append(0.5 * rows(o[5], o[6]))
    off = 0
    for p in parts:
        main_ref[0, :, off:off + p.shape[0]] = p.T.astype(BF16)
        off += p.shape[0]
    cv_t = rows(o[3], o[4])
    cvt_ref[0] = cv_t.astype(BF16)
    cv_ref[0] = cv_t.T.astype(BF16)


def _split_w_in(w):
    nl, d, n_in = w.shape
    rb = 256
    n_main = sum(c for _, c in W_MAIN_COLS.values())
    return pl.pallas_call(
        _split_w_body,
        out_shape=[jax.ShapeDtypeStruct((nl, d, n_main), BF16), jax.ShapeDtypeStruct((nl, d, 2 * CONV_DIM), BF16),
                   jax.ShapeDtypeStruct((nl, 2 * CONV_DIM, d), BF16)],
        grid=(nl, d // rb),
        in_specs=[pl.BlockSpec((1, n_in, rb), lambda l, i: (l, 0, i))],
        out_specs=[pl.BlockSpec((1, rb, n_main), lambda l, i: (l, i, 0)),
                   pl.BlockSpec((1, rb, 2 * CONV_DIM), lambda l, i: (l, i, 0)),
                   pl.BlockSpec((1, 2 * CONV_DIM, rb), lambda l, i: (l, 0, i))],
        compiler_params=_params(2),
        name="split_w_in",
    )(jnp.swapaxes(w, 1, 2))


def kernel(x, c, ctx, c_ctx, w_ada, b_ada, norm1_g, norm2_g, w_in, ssd_conv_w, ssd_conv_b, ssd_a_log,
           ssd_dt_bias, ssd_d, ssd_norm_g, ssd_w_out, cv_dw_w, cv_dw_b, cv_ln_g, cv_ln_b, cv_w_out, pool_w,
           pool_scale, w_out, ffn_w_gate, ffn_w_up, ffn_w_down, final_g):
    nb, t, d = x.shape
    tc = ctx.shape[1]
    depth = w_in.shape[0]
    rows = t // GRID_W

    n_cond = -(-(nb + 1) // 8) * 8
    cc = jnp.concatenate([c, c_ctx[None, :], jnp.zeros((n_cond - nb - 1, d), F32)], axis=0)
    mods = _ada(cc, w_ada, b_ada)

    tm_lat = min(512, t)
    tm_ctx = min(256, tc)
    ts_lat = min(1024, t)
    ts_ctx = min(256, tc)
    cb_lat = min(16, GRID_W)
    lat_row = lambda i: i // (t // tm_lat)
    ctx_row = lambda i: nb

    w_main, w_cv, w_cv_t = _split_w_in(w_in)
    seg = lambda name, arr=0, tr=False: (arr,) + W_MAIN_COLS[name] + (tr,)
    w_fg, w_fu, w_fd = (v.astype(BF16) for v in (ffn_w_gate, ffn_w_up, ffn_w_down))
    row3 = lambda v: v.reshape(depth, 1, v.shape[-1])
    norm1, norm2 = row3(norm1_g), row3(norm2_g)
    ln_rows = lambda v: jnp.broadcast_to(v[:, :, None], (depth, CONV_DIM, LANES))
    merge_prm = dict(
        pool_scale=row3(pool_scale), ln_g_rows=ln_rows(cv_ln_g), ln_b_rows=ln_rows(cv_ln_b),
        w_ssd=ssd_w_out.astype(BF16), w_cv=cv_w_out.astype(BF16),
        w_pool=jnp.pad(pool_w, ((0, 0), (0, 0), (0, POOL_PAD - POOL_GROUP_DIM), (0, 0))).astype(BF16),
        w_out=w_out.astype(BF16))
    ssd_prm = _ssd_params(ssd_a_log, ssd_dt_bias, ssd_d, ssd_norm_g)
    conv5 = (jnp.pad(ssd_conv_w, ((0, 0), (0, 8 - SSD_CONV), (0, 0))), row3(ssd_conv_b))
    cv_vpu_prm = (jnp.pad(cv_dw_w, ((0, 0), (0, 32 - CONV_WIDTH), (0, 0))), row3(cv_dw_b), row3(cv_ln_g), row3(cv_ln_b))
    toe = _toeplitz_pairs(cv_dw_w)
    cv_bias = jnp.broadcast_to(cv_dw_b.reshape(-1)[:, None, None], (depth * CONV_DIM, 1, LANES))

    xl = x.reshape(nb * t, d)
    xc = ctx.reshape(nb * tc, d)
    zero_state = jnp.zeros((nb, 2, SSD_STATE, SSD_INNER), F32)

    for l in range(depth):
        last = l == depth - 1

        if last:
            xbc_c, dt_c = _inproj(xc, l, norm1, mods, ctx_row, [w_main], [seg("xbc"), seg("dt")], [BF16, F32],
                                  tm_ctx, tc, 0, *conv5)
            z_c = jnp.zeros((nb, tc, SSD_INNER), BF16)
        else:
            z_c, xbc_c, dt_c, cv_c, pool_c, gates_c = _inproj(
                xc, l, norm1, mods, ctx_row, [w_main, w_cv],
                [seg("z"), seg("xbc"), seg("dt"), (1, 0, 2 * CONV_DIM, False), seg("pool"), seg("gates")],
                [BF16, BF16, F32, BF16, BF16, BF16], tm_ctx, tc, 1, *conv5)
            z_c = z_c.reshape(nb, tc, SSD_INNER)
        y_c, ctx_state = _ssd(xbc_c.reshape(nb, tc, SSD_XBC), dt_c.reshape(nb, tc, DT_PAD), z_c, l, ssd_prm,
                              zero_state, ts_ctx)
        if not last:
            uc_c = _conv_branch(cv_c, tc, l, *cv_vpu_prm)
            pp_c = _pool_seq(pool_c, tc)
            xc = _merge(xc, y_c.reshape(nb * tc, SSD_INNER), uc_c, pp_c, gates_c, l, mods, ctx_row, merge_prm,
                        tm_ctx, False)
            xc = _ffn(xc, l, norm2, mods, ctx_row, w_fg, w_fu, w_fd, final_g, False, tm_ctx)

        z_l, xbc_l, dt_l, cv_l, pool_l, gates_l = _inproj(
            xl, l, norm1, mods, lat_row, [w_main, w_cv_t],
            [seg("z"), seg("xbc"), seg("dt"), (1, 0, 2 * CONV_DIM, True), seg("pool"), seg("gates")],
            [BF16, BF16, F32, BF16, BF16, BF16], tm_lat, t, 1, *conv5)
        y_l, _ = _ssd(xbc_l.reshape(nb, t, SSD_XBC), dt_l.reshape(nb, t, DT_PAD), z_l.reshape(nb, t, SSD_INNER),
                      l, ssd_prm, ctx_state, ts_lat)
        v_l = _conv_branch_mxu(cv_l, l, toe, cv_bias)
        pp_l = _pool_branch(pool_l.reshape(nb, rows, GRID_W, POOL_PADDED), rows, cb_lat).reshape(-1, POOL_PADDED)
        xl = _merge(xl, y_l.reshape(nb * t, SSD_INNER), v_l, pp_l, gates_l, l, mods, lat_row, merge_prm,
                    tm_lat, True)
        xl = _ffn(xl, l, norm2, mods, lat_row, w_fg, w_fu, w_fd, final_g, last, tm_lat)

    return xl.reshape(nb, t, d)
```

```python
import functools

import numpy as np
import jax
import jax.numpy as jnp
from jax import lax
from jax.experimental import pallas as pl
from jax.experimental.pallas import tpu as pltpu

F32 = jnp.float32
BF16 = jnp.bfloat16

D_MODEL = 1024
GRID_W = 64
EPS = 1e-6
N_MOD = 6
SSD_HEADS = 16
SSD_HEAD_DIM = 64
SSD_INNER = SSD_HEADS * SSD_HEAD_DIM
SSD_GROUPS = 2
SSD_STATE = 128
SSD_CONV = 5
SSD_XBC = SSD_INNER + 2 * SSD_GROUPS * SSD_STATE
CONV_DIM = 768
CONV_WIDTH = 31
POOL_DIM = 768
POOL_WINDOWS = (2, 4, 8, 16)
POOL_GROUPS = len(POOL_WINDOWS)
POOL_GROUP_DIM = POOL_DIM // POOL_GROUPS
POOL_OUT_DIM = D_MODEL // POOL_GROUPS
FFN_HIDDEN = 2816

LANES = 128
MXU_DIM = 256
CHUNK = 128
LOG2E = 1.4426950408889634
POOL_PAD = 256
POOL_PADDED = POOL_GROUPS * POOL_PAD
DT_PAD = 2 * LANES
HALO = 16
VMEM_LIMIT = 56 * 1024 * 1024


def _layer_spec(arr, l):
    shape = arr.shape[1:]
    nd = len(shape)
    return pl.BlockSpec((1,) + shape, lambda *_: (l,) + (0,) * nd, pipeline_mode=pl.Buffered(1))


def _mod_spec(l, k, row_of_tile):
    return pl.BlockSpec((1, 1, 1, D_MODEL), lambda i: (l, row_of_tile(i), 0, k))


def _params(n_axes):
    return pltpu.CompilerParams(dimension_semantics=("arbitrary",) * n_axes, vmem_limit_bytes=VMEM_LIMIT)


def _sigmoid(v):
    return 0.5 * jnp.tanh(0.5 * v) + 0.5


def _silu(v):
    return v * _sigmoid(v)


def _dot(a, b):
    return jnp.dot(a, b, preferred_element_type=F32)


def _dot_nt(a, b):
    return lax.dot_general(a, b, (((1,), (1,)), ((), ())), preferred_element_type=F32)


def _ada_body(c_ref, w_ref, b_ref, o_ref):
    s = _silu(c_ref[...])
    o_ref[0] = jnp.dot(s, w_ref[0], preferred_element_type=F32,
                       precision=lax.Precision.HIGHEST) + b_ref[0]


def _ada(cc, w_ada, b_ada):
    nl, d, nm = w_ada.shape
    r = cc.shape[0]
    tn = 1024
    return pl.pallas_call(
        _ada_body,
        out_shape=jax.ShapeDtypeStruct((nl, r, nm), F32),
        grid=(nl, nm // tn),
        in_specs=[pl.BlockSpec((r, d), lambda l, j: (0, 0)),
                  pl.BlockSpec((1, d, tn), lambda l, j: (l, 0, j)),
                  pl.BlockSpec((1, 1, tn), lambda l, j: (l, 0, j))],
        out_specs=pl.BlockSpec((1, r, tn), lambda l, j: (l, 0, j)),
        compiler_params=_params(2),
        name="adaln",
    )(cc, w_ada, b_ada.reshape(nl, 1, nm)).reshape(nl, r, 1, nm)


def _norm_mod(x, g, shift, scale):
    ms = jnp.mean(x * x, axis=-1, keepdims=True)
    h = (x * lax.rsqrt(ms + EPS)) * g
    return h * (1.0 + scale) + shift


def _inproj_body(*refs, n_arr, segs, col_chunk, conv_seg, tiles_per_seq):
    x_ref, xp_ref, xn_ref, g_ref, sh_ref, sc_ref, cw_ref, cb_ref = refs[:8]
    w_refs = refs[8:8 + n_arr]
    o_refs = refs[8 + n_arr:]
    tm = x_ref.shape[0]
    norm = lambda v: _norm_mod(v, g_ref[0], sh_ref[0, 0], sc_ref[0, 0]).astype(BF16)
    hb = norm(x_ref[...])
    pos = pl.program_id(0) % tiles_per_seq
    h_prev = jnp.where(pos > 0, norm(xp_ref[...]), jnp.zeros((HALO, x_ref.shape[1]), BF16))
    h_next = jnp.where(pos < tiles_per_seq - 1, norm(xn_ref[...]), jnp.zeros((HALO, x_ref.shape[1]), BF16))
    hb_ext = jnp.concatenate([h_prev, hb, h_next], axis=0)
    for seg, ((arr, off, n, tr), o_ref) in enumerate(zip(segs, o_refs)):
        w_ref = w_refs[arr]
        step = min(col_chunk, n)
        for j in range(0, n, step):
            if tr:
                o_ref[j:j + step, :] = _dot_nt(w_ref[0, off + j:off + j + step, :], hb).astype(o_ref.dtype)
            elif seg == conv_seg:
                r = _dot(hb_ext, w_ref[0, :, off + j:off + j + step])
                rows = tm + 2 * HALO
                acc = jnp.broadcast_to(cb_ref[0, :, j:j + step], (tm, step))
                for k in range(SSD_CONV):
                    sh = (SSD_CONV // 2 - k) % rows
                    rk = r if sh == 0 else pltpu.roll(r, sh, axis=0)
                    acc = acc + rk[HALO:HALO + tm] * cw_ref[0, k:k + 1, j:j + step]
                hv = 0.5 * acc
                o_ref[:, j:j + step] = (hv + hv * jnp.tanh(hv)).astype(o_ref.dtype)
            else:
                o_ref[:, j:j + step] = _dot(hb, w_ref[0, :, off + j:off + j + step]).astype(o_ref.dtype)


def _inproj(x, l, norm_g, mods, row_of_tile, arrays, segs, out_dtypes, tm, seq_len, conv_seg, conv_w, conv_b):
    n, d = x.shape
    out_shape, out_specs = [], []
    for (_, _, cols, tr), dt in zip(segs, out_dtypes):
        if tr:
            out_shape.append(jax.ShapeDtypeStruct((cols, n), dt))
            out_specs.append(pl.BlockSpec((cols, tm), lambda i: (0, i)))
        else:
            out_shape.append(jax.ShapeDtypeStruct((n, cols), dt))
            out_specs.append(pl.BlockSpec((tm, cols), lambda i: (i, 0)))
    hb = tm // HALO
    return pl.pallas_call(
        functools.partial(_inproj_body, n_arr=len(arrays), segs=tuple(segs), col_chunk=512, conv_seg=conv_seg,
                          tiles_per_seq=seq_len // tm),
        out_shape=out_shape,
        grid=(n // tm,),
        in_specs=[pl.BlockSpec((tm, d), lambda i: (i, 0)),
                  pl.BlockSpec((HALO, d), lambda i: (jnp.maximum(i * hb - 1, 0), 0)),
                  pl.BlockSpec((HALO, d), lambda i: (jnp.minimum((i + 1) * hb, n // HALO - 1), 0)),
                  _layer_spec(norm_g, l), _mod_spec(l, 0, row_of_tile), _mod_spec(l, 1, row_of_tile),
                  _layer_spec(conv_w, l), _layer_spec(conv_b, l)]
                 + [_layer_spec(w, l) for w in arrays],
        out_specs=out_specs,
        compiler_params=_params(1),
        name="inproj",
    )(x, x, x, norm_g, mods, mods, conv_w, conv_b, *arrays)


def _split3(v):
    hi = v.astype(BF16)
    r1 = v - hi.astype(F32)
    mid = r1.astype(BF16)
    lo = (r1 - mid.astype(F32)).astype(BF16)
    return hi, mid, lo


def _ssd_body(xm_ref, dt_ref, z_ref, alog_ref, dtb_ref, dsk_ref,
              ng_ref, init_ref, y_ref, st_ref, yf_all, pt_ref, ec_ref, *, n_steps, ts):
    s = pl.program_id(1)
    fwd = s < n_steps
    c = jnp.where(fwd, s, 2 * n_steps - 1 - s)
    base = pl.multiple_of(c * ts, ts)
    n_sub = ts // CHUNK
    nh = SSD_HEADS
    inner = SSD_INNER
    half_grp = SSD_GROUPS * SSD_STATE
    pairs_per_group = SSD_HEADS // SSD_GROUPS // 2

    @pl.when(jnp.logical_or(s == 0, s == n_steps))
    def _():
        st_ref[...] = init_ref[...]

    @pl.when(fwd)
    def _():
        yf_all[pl.ds(base, ts), :] = xm_ref[0, :, :inner].astype(F32) * dsk_ref[0]

    li = lax.broadcasted_iota(jnp.int32, (CHUNK, CHUNK), 0)
    si = lax.broadcasted_iota(jnp.int32, (CHUNK, CHUNK), 1)
    tri = jnp.where(fwd, li - si, si - li) >= 0
    first_half = si < SSD_HEAD_DIM
    first_half_row = first_half[0:1]

    dtr = dt_ref[0]
    dtr_t = jnp.concatenate([dtr[j * CHUNK:(j + 1) * CHUNK].T[:nh] for j in range(n_sub)], axis=0)
    dtr_t = dtr_t + jnp.tile(dtb_ref[0, 0], (n_sub, 1))
    dt_t = jnp.maximum(dtr_t, 0.0) + jnp.log(1.0 + jnp.exp(-jnp.abs(dtr_t)))
    a2_t = dt_t * jnp.tile(-jnp.exp(alog_ref[0, 0]) * LOG2E, (n_sub, 1))
    order = jnp.where(tri, 1.0, 0.0).astype(BF16)
    pieces = _split3(a2_t)
    e2_t = sum(_dot_nt(p, order) for p in pieces)
    tot2 = jnp.sum(a2_t, axis=1, keepdims=True)
    er2_t = e2_t - jnp.log(dt_t) * LOG2E
    pt_ref[0] = er2_t
    pt_ref[1] = jnp.exp2(tot2 - er2_t)
    pt_ref[2] = jnp.broadcast_to(jnp.exp2(tot2), e2_t.shape)
    zpad = jnp.zeros((CHUNK - nh, CHUNK), BF16)
    for j in range(n_sub):
        ec_ref[j * CHUNK:(j + 1) * CHUNK, :] = sum(
            _dot_nt(order, jnp.concatenate([p[j * nh:(j + 1) * nh], zpad], axis=0)) for p in pieces)

    for j in range(n_sub):
        jj = jnp.where(fwd, j, n_sub - 1 - j)
        loc = pl.multiple_of(jj * CHUNK, CHUNK)
        off = pl.multiple_of(base + loc, CHUNK)
        hrow = pl.multiple_of(jj * nh, nh)
        xc = xm_ref[0, pl.ds(loc, CHUNK), :]
        er2 = pt_ref[0, pl.ds(hrow, nh), :]
        w_t = pt_ref[1, pl.ds(hrow, nh), :]
        etot = pt_ref[2, pl.ds(hrow, nh), :]
        e_col = ec_ref[pl.ds(loc, CHUNK), :]
        y_pairs = []
        for g in range(SSD_GROUPS):
            b_g = xc[:, inner + g * SSD_STATE: inner + (g + 1) * SSD_STATE]
            c_g = xc[:, inner + half_grp + g * SSD_STATE: inner + half_grp + (g + 1) * SSD_STATE]
            cb = _dot_nt(c_g, b_g).astype(BF16)
            b_t = b_g.astype(F32).T.astype(BF16)
            gcols = slice(g * inner // SSD_GROUPS, (g + 1) * inner // SSD_GROUPS)
            y_carry = _dot(c_g, st_ref[0, 0, :, gcols].astype(BF16))
            for pr in range(pairs_per_group):
                pair = g * pairs_per_group + pr
                cols = slice(pair * LANES, (pair + 1) * LANES)
                xp = xc[:, cols]
                zero = jnp.zeros_like(xp)
                x12 = jnp.concatenate([jnp.where(first_half, xp, zero), jnp.where(first_half, zero, xp)], axis=0)
                ms, bw, dec = [], [], []
                for h in (2 * pair, 2 * pair + 1):
                    ecol = jnp.broadcast_to(e_col[:, h:h + 1], (CHUNK, CHUNK))
                    ms.append(jnp.exp2(jnp.where(tri, ecol - er2[h:h + 1, :], -1e30)).astype(BF16) * cb)
                    dec.append(jnp.exp2(ecol))
                    w_rows = jnp.broadcast_to(w_t[h:h + 1, :], (2 * 8, CHUNK)).astype(BF16)
                    bw.append(b_t * jnp.tile(w_rows, (SSD_STATE // (2 * 8), 1)))
                both = _dot(jnp.concatenate([jnp.concatenate(ms, axis=1), jnp.concatenate(bw, axis=1)], axis=0), x12)
                y_pairs.append(both[:CHUNK] + jnp.where(first_half, dec[0], dec[1]) * y_carry[:, pr * LANES:(pr + 1) * LANES])
                erow = jnp.where(first_half_row, etot[2 * pair:2 * pair + 1, :], etot[2 * pair + 1:2 * pair + 2, :])
                st_ref[0, 0, :, cols] = st_ref[0, 0, :, cols] * erow + both[CHUNK:]
        yf_all[pl.ds(off, CHUNK), :] = yf_all[pl.ds(off, CHUNK), :] + jnp.concatenate(y_pairs, axis=1)

    @pl.when(jnp.logical_not(fwd))
    def _():
        y = yf_all[pl.ds(base, ts), :] * _silu(z_ref[0].astype(F32))
        gw = inner // SSD_GROUPS
        for g in range(SSD_GROUPS):
            yg = y[:, g * gw:(g + 1) * gw]
            ms = jnp.mean(yg * yg, axis=-1, keepdims=True)
            yn = (yg * lax.rsqrt(ms + EPS)) * ng_ref[0, :, g * gw:(g + 1) * gw]
            y_ref[0, :, g * gw:(g + 1) * gw] = yn.astype(y_ref.dtype)


def _ssd(xbc, dt, z, l, prm, init, ts):
    nb, t, _ = xbc.shape
    n_steps = t // ts

    def tok_out(s):
        return jnp.where(s < n_steps, n_steps - 1, 2 * n_steps - 1 - s)

    def tok_dt(s):
        return jnp.where(s < n_steps, s, 2 * n_steps - 1 - s)

    def phase(s):
        return jnp.where(s < n_steps, 0, 1)

    head_spec = pl.BlockSpec((1, 1, SSD_HEADS, LANES), lambda b, s: (l, phase(s), 0, 0))
    return pl.pallas_call(
        functools.partial(_ssd_body, n_steps=n_steps, ts=ts),
        out_shape=[jax.ShapeDtypeStruct((nb, t, SSD_INNER), BF16),
                   jax.ShapeDtypeStruct((nb, 2, SSD_STATE, SSD_INNER), F32)],
        grid=(nb, 2 * n_steps),
        in_specs=[
            pl.BlockSpec((1, ts, SSD_XBC), lambda b, s: (b, tok_dt(s), 0)),
            pl.BlockSpec((1, ts, LANES), lambda b, s: (b, tok_dt(s), phase(s))),
            pl.BlockSpec((1, ts, SSD_INNER), lambda b, s: (b, tok_out(s), 0)),
            head_spec,
            head_spec,
            _layer_spec(prm["d_full"], l),
            _layer_spec(prm["norm_g"], l),
            pl.BlockSpec((1, 1, SSD_STATE, SSD_INNER), lambda b, s: (b, phase(s), 0, 0)),
        ],
        out_specs=[
            pl.BlockSpec((1, ts, SSD_INNER), lambda b, s: (b, tok_out(s), 0)),
            pl.BlockSpec((1, 1, SSD_STATE, SSD_INNER), lambda b, s: (b, phase(s), 0, 0)),
        ],
        scratch_shapes=[pltpu.VMEM((t, SSD_INNER), F32),
                        pltpu.VMEM((3, (ts // CHUNK) * SSD_HEADS, CHUNK), F32), pltpu.VMEM((ts, LANES), F32)],
        compiler_params=_params(2),
        name="ssd_scan",
    )(xbc, dt, z, prm["a_log"], prm["dt_bias"], prm["d_full"], prm["norm_g"], init)


def _ssd_params(a_log, dt_bias, d_skip, norm_g):
    nl = a_log.shape[0]
    rep = lambda v: jnp.broadcast_to(v[:, :, :, None], (nl, 2, SSD_HEADS, LANES))
    return dict(
        a_log=rep(a_log), dt_bias=rep(dt_bias),
        d_full=jnp.repeat(d_skip, SSD_HEAD_DIM, axis=1).reshape(nl, 1, SSD_INNER),
        norm_g=norm_g.reshape(nl, 1, SSD_INNER))


def _toeplitz_blocks(dw_w, width):
    half = CONV_WIDTH // 2
    s_idx, t_idx = np.meshgrid(np.arange(LANES), np.arange(LANES), indexing="ij")
    row = min(width, LANES)
    sel = np.zeros((3, CONV_WIDTH, LANES, LANES), np.float32)
    for j, off in enumerate((0, -LANES, LANES)):
        same_row = ((s_idx // row) == (t_idx // row)) if j == 0 else np.ones_like(s_idx, bool)
        for k in range(CONV_WIDTH):
            sel[j, k] = (s_idx + off - t_idx + half == k) & same_row
    taps = jnp.swapaxes(dw_w, 1, 2).reshape(-1, CONV_WIDTH).astype(BF16)
    return jnp.einsum("ck,jkst->cjst", taps, jnp.asarray(sel, BF16), preferred_element_type=BF16)


def _cvmx_body(a_ref, b_ref, t_ref, bias_ref, o_ref, *, n_ch, blocks_per_row):
    rb = a_ref.shape[1]
    pos = lax.broadcasted_iota(jnp.int32, (rb, LANES), 0) % blocks_per_row

    def ch_body(c, carry):
        u = a_ref[c].astype(F32) * _sigmoid(b_ref[c].astype(F32))
        v = _dot(u.astype(BF16), t_ref[c, 0])
        if blocks_per_row > 1:
            u_prev = jnp.where(pos > 0, pltpu.roll(u, 1, axis=0), 0.0)
            u_next = jnp.where(pos < blocks_per_row - 1, pltpu.roll(u, rb - 1, axis=0), 0.0)
            v = v + _dot(u_prev.astype(BF16), t_ref[c, 1]) + _dot(u_next.astype(BF16), t_ref[c, 2])
        o_ref[c] = (v + bias_ref[c]).astype(o_ref.dtype)
        return carry

    lax.fori_loop(0, n_ch, ch_body, 0, unroll=8)


def _conv_branch_mxu(cv_t, width, l, toe, bias):
    n = cv_t.shape[1]
    sr = n // LANES
    rb = min(256, sr)
    cbk = 64
    x3 = cv_t.reshape(2 * CONV_DIM, sr, LANES)
    nblk = CONV_DIM // cbk
    out = pl.pallas_call(
        functools.partial(_cvmx_body, n_ch=cbk, blocks_per_row=max(1, width // LANES)),
        out_shape=jax.ShapeDtypeStruct((CONV_DIM, sr, LANES), BF16),
        grid=(nblk, sr // rb),
        in_specs=[pl.BlockSpec((cbk, rb, LANES), lambda c, i: (c, i, 0)),
                  pl.BlockSpec((cbk, rb, LANES), lambda c, i: (c + nblk, i, 0)),
                  pl.BlockSpec((cbk, 3, LANES, LANES), lambda c, i: (l * nblk + c, 0, 0, 0)),
                  pl.BlockSpec((cbk, 1, LANES), lambda c, i: (l * nblk + c, 0, 0))],
        out_specs=pl.BlockSpec((cbk, rb, LANES), lambda c, i: (c, i, 0)),
        compiler_params=_params(2),
        name="conv_branch_mxu",
    )(x3, x3, toe, bias)
    return out.reshape(CONV_DIM, n)


def _pool_body(x_ref, ic_ref, o_ref, *, n):
    max_lo = max(POOL_WINDOWS) // 2
    for k, w in enumerate(POOL_WINDOWS):
        ls = slice(k * POOL_PAD, (k + 1) * POOL_PAD)
        x = x_ref[0, :, :, ls].astype(F32)
        zeros = jnp.zeros((max_lo,) + x.shape[1:], F32)
        run = jnp.concatenate([zeros, x, zeros], axis=0)
        span = 1
        while span < w:
            run = run[:-span] + run[span:]
            span *= 2
        lo = w // 2
        total = run[max_lo - lo:max_lo - lo + n]
        o_ref[0, :, :, ls] = (total * ic_ref[:, :, ls] - x).astype(o_ref.dtype)


def _pool_branch(pool_in, n, cb):
    nb, _, c, _ = pool_in.shape
    pos = np.arange(n)
    inv = np.zeros((n, 1, POOL_PADDED), np.float32)
    for k, w in enumerate(POOL_WINDOWS):
        lo = w // 2
        cnt = np.clip(pos + (w - 1 - lo) + 1, 0, n) - np.clip(pos - lo, 0, n)
        inv[:, 0, k * POOL_PAD:(k + 1) * POOL_PAD] = (1.0 / cnt)[:, None]
    return pl.pallas_call(
        functools.partial(_pool_body, n=n),
        out_shape=jax.ShapeDtypeStruct(pool_in.shape, BF16),
        grid=(nb, c // cb),
        in_specs=[pl.BlockSpec((1, n, cb, POOL_PADDED), lambda b, j: (b, 0, j, 0)),
                  pl.BlockSpec((n, 1, POOL_PADDED), lambda b, j: (0, 0, 0), pipeline_mode=pl.Buffered(1))],
        out_specs=pl.BlockSpec((1, n, cb, POOL_PADDED), lambda b, j: (b, 0, j, 0)),
        compiler_params=_params(2),
        name="pool_branch",
    )(pool_in, jnp.asarray(inv))


def _pool_seq_body(x_ref, band_ref, ic_ref, o_ref):
    for k in range(POOL_GROUPS):
        ls = slice(k * POOL_PAD, (k + 1) * POOL_PAD)
        x = x_ref[:, ls]
        o_ref[:, ls] = (_dot(band_ref[k], x) * ic_ref[k] - x.astype(F32)).astype(o_ref.dtype)


def _pool_seq(pool_in, n):
    pos = np.arange(n)
    band = np.zeros((POOL_GROUPS, n, n), np.float32)
    inv = np.zeros((POOL_GROUPS, n, POOL_PAD), np.float32)
    for k, w in enumerate(POOL_WINDOWS):
        lo = w // 2
        inside = (pos[None, :] >= pos[:, None] - lo) & (pos[None, :] <= pos[:, None] + (w - 1 - lo))
        band[k] = inside
        inv[k] = (1.0 / inside.sum(axis=1))[:, None]
    whole = lambda shape: pl.BlockSpec(shape, lambda b: (0,) * len(shape), pipeline_mode=pl.Buffered(1))
    return pl.pallas_call(
        _pool_seq_body,
        out_shape=jax.ShapeDtypeStruct(pool_in.shape, BF16),
        grid=(pool_in.shape[0] // n,),
        in_specs=[pl.BlockSpec((n, POOL_PADDED), lambda b: (b, 0)), whole(band.shape), whole(inv.shape)],
        out_specs=pl.BlockSpec((n, POOL_PADDED), lambda b: (b, 0)),
        compiler_params=_params(1),
        name="pool_seq",
    )(pool_in, jnp.asarray(band, BF16), jnp.asarray(inv))


def _merge_body(x_ref, ys_ref, uc_ref, pp_ref, gt_ref, mg_ref, ps_ref, lg_ref, lb_ref, wss_ref, wcv_ref, wpl_ref,
                wo_ref, o_ref):
    d = D_MODEL
    y_ssd = _dot(ys_ref[...], wss_ref[0])
    v = uc_ref[...].astype(F32)
    mu = jnp.mean(v, axis=0, keepdims=True)
    dv = v - mu
    var = jnp.mean(dv * dv, axis=0, keepdims=True)
    reps = v.shape[1] // LANES
    yn = dv * lax.rsqrt(var + EPS) * jnp.tile(lg_ref[0], (1, reps)) + jnp.tile(lb_ref[0], (1, reps))
    y_cv = lax.dot_general(_silu(yn).astype(BF16), wcv_ref[0], (((0,), (0,)), ((), ())), preferred_element_type=F32)
    y_pool = jnp.concatenate(
        [_dot(pp_ref[:, k * POOL_PAD:(k + 1) * POOL_PAD], wpl_ref[0, k]) for k in range(POOL_GROUPS)], axis=1)
    y_pool = y_pool * ps_ref[0]
    merged2 = ((jnp.tanh(gt_ref[:, 0:d].astype(F32)) + 1.0) * y_ssd
               + (jnp.tanh(gt_ref[:, d:2 * d].astype(F32)) + 1.0) * y_cv
               + (jnp.tanh(gt_ref[:, 2 * d:3 * d].astype(F32)) + 1.0) * y_pool)
    o_ref[...] = x_ref[...] + (0.5 * mg_ref[0, 0]) * _dot(merged2.astype(BF16), wo_ref[0])


def _merge(x, ys, uc, pp, gates, l, mods, row_of_tile, prm, tm):
    n, d = x.shape
    tok = lambda w: pl.BlockSpec((tm, w), lambda i: (i, 0))
    names = ("pool_scale", "ln_g_rows", "ln_b_rows", "w_ssd", "w_cv", "w_pool", "w_out")
    return pl.pallas_call(
        _merge_body,
        out_shape=jax.ShapeDtypeStruct((n, d), F32),
        grid=(n // tm,),
        in_specs=[tok(d), tok(SSD_INNER), pl.BlockSpec((CONV_DIM, tm), lambda i: (0, i)), tok(POOL_PADDED),
                  tok(3 * d), _mod_spec(l, 2, row_of_tile)] + [_layer_spec(prm[k], l) for k in names],
        out_specs=tok(d),
        compiler_params=_params(1),
        name="merge",
    )(x, ys, uc, pp, gates, mods, *[prm[k] for k in names])


def _ffn_body(x_ref, g_ref, sh_ref, sc_ref, gt_ref, wg_ref, wu_ref, wd_ref, fg_ref, o_ref, *, final, hid_chunk):
    x = x_ref[...]
    hb = _norm_mod(x, g_ref[0], sh_ref[0, 0], sc_ref[0, 0]).astype(BF16)
    acc = jnp.zeros(x.shape, F32)
    for j in range(0, FFN_HIDDEN, hid_chunk):
        je = min(j + hid_chunk, FFN_HIDDEN)
        gate = _dot(hb, wg_ref[0, :, j:je])
        up = _dot(hb, wu_ref[0, :, j:je])
        acc = acc + _dot((_silu(gate) * up).astype(BF16), wd_ref[0, j:je, :])
    out = x + gt_ref[0, 0] * acc
    if final:
        ms = jnp.mean(out * out, axis=-1, keepdims=True)
        out = (out * lax.rsqrt(ms + EPS)) * fg_ref[...]
    o_ref[...] = out


def _ffn(x, l, norm_g, mods, row_of_tile, w_gate, w_up, w_down, final_g, final, tm):
    n, d = x.shape
    return pl.pallas_call(
        functools.partial(_ffn_body, final=final, hid_chunk=6 * MXU_DIM),
        out_shape=jax.ShapeDtypeStruct((n, d), F32),
        grid=(n // tm,),
        in_specs=[pl.BlockSpec((tm, d), lambda i: (i, 0)), _layer_spec(norm_g, l),
                  _mod_spec(l, 3, row_of_tile), _mod_spec(l, 4, row_of_tile), _mod_spec(l, 5, row_of_tile),
                  _layer_spec(w_gate, l), _layer_spec(w_up, l), _layer_spec(w_down, l),
                  pl.BlockSpec((1, d), lambda i: (0, 0), pipeline_mode=pl.Buffered(1))],
        out_specs=pl.BlockSpec((tm, d), lambda i: (i, 0)),
        compiler_params=_params(1),
        name="ffn",
    )(x, norm_g, mods, mods, mods, w_gate, w_up, w_down, final_g.reshape(1, d))


W_MAIN_COLS = dict(z=(0, SSD_INNER), xbc=(SSD_INNER, SSD_XBC), dt=(SSD_INNER + SSD_XBC, DT_PAD),
                   pool=(SSD_INNER + SSD_XBC + DT_PAD, POOL_PADDED),
                   gates=(SSD_INNER + SSD_XBC + DT_PAD + POOL_PADDED, 3 * D_MODEL))


def _split_w_body(w_ref, main_ref, cvt_ref):
    o = [int(v) for v in np.cumsum([0, SSD_INNER, SSD_XBC, 2 * SSD_HEADS, 2 * CONV_DIM, POOL_DIM, 3 * D_MODEL])]
    rb = w_ref.shape[2]
    rows = lambda a, b: w_ref[0, a:b, :]

    def padded(a, b, n):
        return jnp.concatenate([rows(a, b), jnp.zeros((n - (b - a), rb), F32)], axis=0)

    parts = [rows(o[0], o[2])]
    for k in range(2):
        parts.append(padded(o[2] + k * SSD_HEADS, o[2] + (k + 1) * SSD_HEADS, LANES))
    for k in range(POOL_GROUPS):
        parts.append(padded(o[4] + k * POOL_GROUP_DIM, o[4] + (k + 1) * POOL_GROUP_DIM, POOL_PAD))
    parts.append(0.5 * rows(o[5], o[6]))
    off = 0
    for p in parts:
        main_ref[0, :, off:off + p.shape[0]] = p.T.astype(BF16)
        off += p.shape[0]
    cvt_ref[0] = rows(o[3], o[4]).astype(BF16)


def _split_w_in(w):
    nl, d, n_in = w.shape
    rb = 256
    n_main = sum(c for _, c in W_MAIN_COLS.values())
    return pl.pallas_call(
        _split_w_body,
        out_shape=[jax.ShapeDtypeStruct((nl, d, n_main), BF16), jax.ShapeDtypeStruct((nl, 2 * CONV_DIM, d), BF16)],
        grid=(nl, d // rb),
        in_specs=[pl.BlockSpec((1, n_in, rb), lambda l, i: (l, 0, i))],
        out_specs=[pl.BlockSpec((1, rb, n_main), lambda l, i: (l, i, 0)),
                   pl.BlockSpec((1, 2 * CONV_DIM, rb), lambda l, i: (l, 0, i))],
        compiler_params=_params(2),
        name="split_w_in",
    )(jnp.swapaxes(w, 1, 2))


def kernel(x, c, ctx, c_ctx, w_ada, b_ada, norm1_g, norm2_g, w_in, ssd_conv_w, ssd_conv_b, ssd_a_log,
           ssd_dt_bias, ssd_d, ssd_norm_g, ssd_w_out, cv_dw_w, cv_dw_b, cv_ln_g, cv_ln_b, cv_w_out, pool_w,
           pool_scale, w_out, ffn_w_gate, ffn_w_up, ffn_w_down, final_g):
    nb, t, d = x.shape
    tc = ctx.shape[1]
    depth = w_in.shape[0]
    rows = t // GRID_W

    n_cond = -(-(nb + 1) // 8) * 8
    cc = jnp.concatenate([c, c_ctx[None, :], jnp.zeros((n_cond - nb - 1, d), F32)], axis=0)
    mods = _ada(cc, w_ada, b_ada)

    tm_lat = min(512, t)
    tm_ctx = min(256, tc)
    ts_lat = min(1024, t)
    ts_ctx = min(256, tc)
    cb_lat = min(16, GRID_W)
    lat_row = lambda i: i // (t // tm_lat)
    ctx_row = lambda i: nb

    w_main, w_cv_t = _split_w_in(w_in)
    seg = lambda name, arr=0, tr=False: (arr,) + W_MAIN_COLS[name] + (tr,)
    w_fg, w_fu, w_fd = (v.astype(BF16) for v in (ffn_w_gate, ffn_w_up, ffn_w_down))
    row3 = lambda v: v.reshape(depth, 1, v.shape[-1])
    norm1, norm2 = row3(norm1_g), row3(norm2_g)
    ln_rows = lambda v: jnp.broadcast_to(v[:, :, None], (depth, CONV_DIM, LANES))
    merge_prm = dict(
        pool_scale=row3(pool_scale), ln_g_rows=ln_rows(cv_ln_g), ln_b_rows=ln_rows(cv_ln_b),
        w_ssd=ssd_w_out.astype(BF16), w_cv=cv_w_out.astype(BF16),
        w_pool=jnp.pad(pool_w, ((0, 0), (0, 0), (0, POOL_PAD - POOL_GROUP_DIM), (0, 0))).astype(BF16),
        w_out=w_out.astype(BF16))
    ssd_prm = _ssd_params(ssd_a_log, ssd_dt_bias, ssd_d, ssd_norm_g)
    conv5 = (jnp.pad(ssd_conv_w, ((0, 0), (0, 8 - SSD_CONV), (0, 0))), row3(ssd_conv_b))
    toe_lat = _toeplitz_blocks(cv_dw_w, GRID_W)
    toe_ctx = _toeplitz_blocks(cv_dw_w, tc) if depth > 1 else None
    cv_bias = jnp.broadcast_to(cv_dw_b.reshape(-1)[:, None, None], (depth * CONV_DIM, 1, LANES))

    xl = x.reshape(nb * t, d)
    xc = ctx.reshape(nb * tc, d)
    zero_state = jnp.zeros((nb, 2, SSD_STATE, SSD_INNER), F32)

    for l in range(depth):
        last = l == depth - 1

        if last:
            xbc_c, dt_c = _inproj(xc, l, norm1, mods, ctx_row, [w_main], [seg("xbc"), seg("dt")], [BF16, F32],
                                  tm_ctx, tc, 0, *conv5)
            z_c = jnp.zeros((nb, tc, SSD_INNER), BF16)
        else:
            z_c, xbc_c, dt_c, cv_c, pool_c, gates_c = _inproj(
                xc, l, norm1, mods, ctx_row, [w_main, w_cv_t],
                [seg("z"), seg("xbc"), seg("dt"), (1, 0, 2 * CONV_DIM, True), seg("pool"), seg("gates")],
                [BF16, BF16, F32, BF16, BF16, BF16], tm_ctx, tc, 1, *conv5)
            z_c = z_c.reshape(nb, tc, SSD_INNER)
        y_c, ctx_state = _ssd(xbc_c.reshape(nb, tc, SSD_XBC), dt_c.reshape(nb, tc, DT_PAD), z_c, l, ssd_prm,
                              zero_state, ts_ctx)
        if not last:
            v_c = _conv_branch_mxu(cv_c, tc, l, toe_ctx, cv_bias)
            pp_c = _pool_seq(pool_c, tc)
            xc = _merge(xc, y_c.reshape(nb * tc, SSD_INNER), v_c, pp_c, gates_c, l, mods, ctx_row, merge_prm, tm_ctx)
            xc = _ffn(xc, l, norm2, mods, ctx_row, w_fg, w_fu, w_fd, final_g, False, tm_ctx)

        z_l, xbc_l, dt_l, cv_l, pool_l, gates_l = _inproj(
            xl, l, norm1, mods, lat_row, [w_main, w_cv_t],
            [seg("z"), seg("xbc"), seg("dt"), (1, 0, 2 * CONV_DIM, True), seg("pool"), seg("gates")],
            [BF16, BF16, F32, BF16, BF16, BF16], tm_lat, t, 1, *conv5)
        y_l, _ = _ssd(xbc_l.reshape(nb, t, SSD_XBC), dt_l.reshape(nb, t, DT_PAD), z_l.reshape(nb, t, SSD_INNER),
                      l, ssd_prm, ctx_state, ts_lat)
        v_l = _conv_branch_mxu(cv_l, GRID_W, l, toe_lat, cv_bias)
        pp_l = _pool_branch(pool_l.reshape(nb, rows, GRID_W, POOL_PADDED), rows, cb_lat).reshape(-1, POOL_PADDED)
        xl = _merge(xl, y_l.reshape(nb * t, SSD_INNER), v_l, pp_l, gates_l, l, mods, lat_row, merge_prm, tm_lat)
        xl = _ffn(xl, l, norm2, mods, lat_row, w_fg, w_fu, w_fd, final_g, last, tm_lat)

    return xl.reshape(nb, t, d)
```

```python
import functools

import numpy as np
import jax
import jax.numpy as jnp
from jax import lax
from jax.experimental import pallas as pl
from jax.experimental.pallas import tpu as pltpu

F32 = jnp.float32
BF16 = jnp.bfloat16

D_MODEL = 1024
GRID_W = 64
EPS = 1e-6
N_MOD = 6
SSD_HEADS = 16
SSD_HEAD_DIM = 64
SSD_INNER = SSD_HEADS * SSD_HEAD_DIM
SSD_GROUPS = 2
SSD_STATE = 128
SSD_CONV = 5
SSD_XBC = SSD_INNER + 2 * SSD_GROUPS * SSD_STATE
CONV_DIM = 768
CONV_WIDTH = 31
POOL_DIM = 768
POOL_WINDOWS = (2, 4, 8, 16)
POOL_GROUPS = len(POOL_WINDOWS)
POOL_GROUP_DIM = POOL_DIM // POOL_GROUPS
POOL_OUT_DIM = D_MODEL // POOL_GROUPS
FFN_HIDDEN = 2816

LANES = 128
MXU_DIM = 256
CHUNK = 128
LOG2E = 1.4426950408889634
POOL_PAD = 256
POOL_PADDED = POOL_GROUPS * POOL_PAD
DT_PAD = 2 * LANES
HALO = 16
CV_PAD = 16
VMEM_LIMIT = 56 * 1024 * 1024


def _layer_spec(arr, l):
    shape = arr.shape[1:]
    nd = len(shape)
    return pl.BlockSpec((1,) + shape, lambda *_: (l,) + (0,) * nd, pipeline_mode=pl.Buffered(1))


def _mod_spec(l, k, row_of_tile):
    return pl.BlockSpec((1, 1, 1, D_MODEL), lambda i: (l, row_of_tile(i), 0, k))


def _params(n_axes):
    return pltpu.CompilerParams(dimension_semantics=("arbitrary",) * n_axes, vmem_limit_bytes=VMEM_LIMIT)


def _sigmoid(v):
    return 0.5 * jnp.tanh(0.5 * v) + 0.5


def _silu(v):
    return v * _sigmoid(v)


def _dot(a, b):
    return jnp.dot(a, b, preferred_element_type=F32)


def _dot_nt(a, b):
    return lax.dot_general(a, b, (((1,), (1,)), ((), ())), preferred_element_type=F32)


def _ada_body(c_ref, w_ref, b_ref, o_ref):
    s = _silu(c_ref[...])
    o_ref[0] = jnp.dot(s, w_ref[0], preferred_element_type=F32,
                       precision=lax.Precision.HIGHEST) + b_ref[0]


def _ada(cc, w_ada, b_ada):
    nl, d, nm = w_ada.shape
    r = cc.shape[0]
    tn = 1024
    return pl.pallas_call(
        _ada_body,
        out_shape=jax.ShapeDtypeStruct((nl, r, nm), F32),
        grid=(nl, nm // tn),
        in_specs=[pl.BlockSpec((r, d), lambda l, j: (0, 0)),
                  pl.BlockSpec((1, d, tn), lambda l, j: (l, 0, j)),
                  pl.BlockSpec((1, 1, tn), lambda l, j: (l, 0, j))],
        out_specs=pl.BlockSpec((1, r, tn), lambda l, j: (l, 0, j)),
        compiler_params=_params(2),
        name="adaln",
    )(cc, w_ada, b_ada.reshape(nl, 1, nm)).reshape(nl, r, 1, nm)


def _norm_mod(x, g, shift, scale):
    ms = jnp.mean(x * x, axis=-1, keepdims=True)
    h = (x * lax.rsqrt(ms + EPS)) * g
    return h * (1.0 + scale) + shift


def _inproj_body(*refs, n_arr, segs, col_chunk, conv_seg, tiles_per_seq):
    x_ref, xp_ref, xn_ref, g_ref, sh_ref, sc_ref, cw_ref, cb_ref = refs[:8]
    w_refs = refs[8:8 + n_arr]
    o_refs = refs[8 + n_arr:]
    tm = x_ref.shape[0]
    norm = lambda v: _norm_mod(v, g_ref[0], sh_ref[0, 0], sc_ref[0, 0]).astype(BF16)
    hb = norm(x_ref[...])
    pos = pl.program_id(0) % tiles_per_seq
    h_prev = jnp.where(pos > 0, norm(xp_ref[...]), jnp.zeros((HALO, x_ref.shape[1]), BF16))
    h_next = jnp.where(pos < tiles_per_seq - 1, norm(xn_ref[...]), jnp.zeros((HALO, x_ref.shape[1]), BF16))
    hb_ext = jnp.concatenate([h_prev, hb, h_next], axis=0)
    for seg, ((arr, off, n, tr), o_ref) in enumerate(zip(segs, o_refs)):
        w_ref = w_refs[arr]
        step = min(col_chunk, n)
        for j in range(0, n, step):
            if tr:
                o_ref[j:j + step, :] = _dot_nt(w_ref[0, off + j:off + j + step, :], hb).astype(o_ref.dtype)
            elif seg == conv_seg:
                r = _dot(hb_ext, w_ref[0, :, off + j:off + j + step])
                rows = tm + 2 * HALO
                acc = jnp.broadcast_to(cb_ref[0, :, j:j + step], (tm, step))
                for k in range(SSD_CONV):
                    sh = (SSD_CONV // 2 - k) % rows
                    rk = r if sh == 0 else pltpu.roll(r, sh, axis=0)
                    acc = acc + rk[HALO:HALO + tm] * cw_ref[0, k:k + 1, j:j + step]
                hv = 0.5 * acc
                o_ref[:, j:j + step] = (hv + hv * jnp.tanh(hv)).astype(o_ref.dtype)
            else:
                o_ref[:, j:j + step] = _dot(hb, w_ref[0, :, off + j:off + j + step]).astype(o_ref.dtype)


def _inproj(x, l, norm_g, mods, row_of_tile, arrays, segs, out_dtypes, tm, seq_len, conv_seg, conv_w, conv_b):
    n, d = x.shape
    out_shape, out_specs = [], []
    for (_, _, cols, tr), dt in zip(segs, out_dtypes):
        if tr:
            out_shape.append(jax.ShapeDtypeStruct((cols, n), dt))
            out_specs.append(pl.BlockSpec((cols, tm), lambda i: (0, i)))
        else:
            out_shape.append(jax.ShapeDtypeStruct((n, cols), dt))
            out_specs.append(pl.BlockSpec((tm, cols), lambda i: (i, 0)))
    hb = tm // HALO
    return pl.pallas_call(
        functools.partial(_inproj_body, n_arr=len(arrays), segs=tuple(segs), col_chunk=512, conv_seg=conv_seg,
                          tiles_per_seq=seq_len // tm),
        out_shape=out_shape,
        grid=(n // tm,),
        in_specs=[pl.BlockSpec((tm, d), lambda i: (i, 0)),
                  pl.BlockSpec((HALO, d), lambda i: (jnp.maximum(i * hb - 1, 0), 0)),
                  pl.BlockSpec((HALO, d), lambda i: (jnp.minimum((i + 1) * hb, n // HALO - 1), 0)),
                  _layer_spec(norm_g, l), _mod_spec(l, 0, row_of_tile), _mod_spec(l, 1, row_of_tile),
                  _layer_spec(conv_w, l), _layer_spec(conv_b, l)]
                 + [_layer_spec(w, l) for w in arrays],
        out_specs=out_specs,
        compiler_params=_params(1),
        name="inproj",
    )(x, x, x, norm_g, mods, mods, conv_w, conv_b, *arrays)


def _split3(v):
    hi = v.astype(BF16)
    r1 = v - hi.astype(F32)
    mid = r1.astype(BF16)
    lo = (r1 - mid.astype(F32)).astype(BF16)
    return hi, mid, lo


def _ssd_body(xm_ref, dt_ref, z_ref, alog_ref, dtb_ref, dsk_ref,
              ng_ref, init_ref, y_ref, st_ref, yf_all, pt_ref, ec_ref, *, n_steps, ts):
    s = pl.program_id(1)
    fwd = s < n_steps
    c = jnp.where(fwd, s, 2 * n_steps - 1 - s)
    base = pl.multiple_of(c * ts, ts)
    n_sub = ts // CHUNK
    nh = SSD_HEADS
    inner = SSD_INNER
    half_grp = SSD_GROUPS * SSD_STATE
    pairs_per_group = SSD_HEADS // SSD_GROUPS // 2

    @pl.when(jnp.logical_or(s == 0, s == n_steps))
    def _():
        st_ref[...] = init_ref[...]

    @pl.when(fwd)
    def _():
        yf_all[pl.ds(base, ts), :] = xm_ref[0, :, :inner].astype(F32) * dsk_ref[0]

    li = lax.broadcasted_iota(jnp.int32, (CHUNK, CHUNK), 0)
    si = lax.broadcasted_iota(jnp.int32, (CHUNK, CHUNK), 1)
    tri = jnp.where(fwd, li - si, si - li) >= 0
    first_half = si < SSD_HEAD_DIM
    first_half_row = first_half[0:1]

    dtr = dt_ref[0]
    dtr_t = jnp.concatenate([dtr[j * CHUNK:(j + 1) * CHUNK].T[:nh] for j in range(n_sub)], axis=0)
    dtr_t = dtr_t + jnp.tile(dtb_ref[0, 0], (n_sub, 1))
    dt_t = jnp.maximum(dtr_t, 0.0) + jnp.log(1.0 + jnp.exp(-jnp.abs(dtr_t)))
    a2_t = dt_t * jnp.tile(-jnp.exp(alog_ref[0, 0]) * LOG2E, (n_sub, 1))
    order = jnp.where(tri, 1.0, 0.0).astype(BF16)
    pieces = _split3(a2_t)
    e2_t = sum(_dot_nt(p, order) for p in pieces)
    tot2 = jnp.sum(a2_t, axis=1, keepdims=True)
    er2_t = e2_t - jnp.log(dt_t) * LOG2E
    pt_ref[0] = er2_t
    pt_ref[1] = jnp.exp2(tot2 - er2_t)
    pt_ref[2] = jnp.broadcast_to(jnp.exp2(tot2), e2_t.shape)
    zpad = jnp.zeros((CHUNK - nh, CHUNK), BF16)
    for j in range(n_sub):
        ec_ref[j * CHUNK:(j + 1) * CHUNK, :] = sum(
            _dot_nt(order, jnp.concatenate([p[j * nh:(j + 1) * nh], zpad], axis=0)) for p in pieces)

    for j in range(n_sub):
        jj = jnp.where(fwd, j, n_sub - 1 - j)
        loc = pl.multiple_of(jj * CHUNK, CHUNK)
        off = pl.multiple_of(base + loc, CHUNK)
        hrow = pl.multiple_of(jj * nh, nh)
        xc = xm_ref[0, pl.ds(loc, CHUNK), :]
        er2 = pt_ref[0, pl.ds(hrow, nh), :]
        w_t = pt_ref[1, pl.ds(hrow, nh), :]
        etot = pt_ref[2, pl.ds(hrow, nh), :]
        e_col = ec_ref[pl.ds(loc, CHUNK), :]
        y_pairs = []
        for g in range(SSD_GROUPS):
            b_g = xc[:, inner + g * SSD_STATE: inner + (g + 1) * SSD_STATE]
            c_g = xc[:, inner + half_grp + g * SSD_STATE: inner + half_grp + (g + 1) * SSD_STATE]
            cb = _dot_nt(c_g, b_g).astype(BF16)
            b_t = b_g.astype(F32).T.astype(BF16)
            gcols = slice(g * inner // SSD_GROUPS, (g + 1) * inner // SSD_GROUPS)
            y_carry = _dot(c_g, st_ref[0, 0, :, gcols].astype(BF16))
            for pr in range(pairs_per_group):
                pair = g * pairs_per_group + pr
                cols = slice(pair * LANES, (pair + 1) * LANES)
                xp = xc[:, cols]
                zero = jnp.zeros_like(xp)
                x12 = jnp.concatenate([jnp.where(first_half, xp, zero), jnp.where(first_half, zero, xp)], axis=0)
                ms, bw, dec = [], [], []
                for h in (2 * pair, 2 * pair + 1):
                    ecol = jnp.broadcast_to(e_col[:, h:h + 1], (CHUNK, CHUNK))
                    ms.append(jnp.exp2(jnp.where(tri, ecol - er2[h:h + 1, :], -1e30)).astype(BF16) * cb)
                    dec.append(jnp.exp2(ecol))
                    w_rows = jnp.broadcast_to(w_t[h:h + 1, :], (2 * 8, CHUNK)).astype(BF16)
                    bw.append(b_t * jnp.tile(w_rows, (SSD_STATE // (2 * 8), 1)))
                both = _dot(jnp.concatenate([jnp.concatenate(ms, axis=1), jnp.concatenate(bw, axis=1)], axis=0), x12)
                y_pairs.append(both[:CHUNK] + jnp.where(first_half, dec[0], dec[1]) * y_carry[:, pr * LANES:(pr + 1) * LANES])
                erow = jnp.where(first_half_row, etot[2 * pair:2 * pair + 1, :], etot[2 * pair + 1:2 * pair + 2, :])
                st_ref[0, 0, :, cols] = st_ref[0, 0, :, cols] * erow + both[CHUNK:]
        yf_all[pl.ds(off, CHUNK), :] = yf_all[pl.ds(off, CHUNK), :] + jnp.concatenate(y_pairs, axis=1)

    @pl.when(jnp.logical_not(fwd))
    def _():
        y = yf_all[pl.ds(base, ts), :] * _silu(z_ref[0].astype(F32))
        gw = inner // SSD_GROUPS
        for g in range(SSD_GROUPS):
            yg = y[:, g * gw:(g + 1) * gw]
            ms = jnp.mean(yg * yg, axis=-1, keepdims=True)
            yn = (yg * lax.rsqrt(ms + EPS)) * ng_ref[0, :, g * gw:(g + 1) * gw]
            y_ref[0, :, g * gw:(g + 1) * gw] = yn.astype(y_ref.dtype)


def _ssd(xbc, dt, z, l, prm, init, ts):
    nb, t, _ = xbc.shape
    n_steps = t // ts

    def tok_out(s):
        return jnp.where(s < n_steps, n_steps - 1, 2 * n_steps - 1 - s)

    def tok_dt(s):
        return jnp.where(s < n_steps, s, 2 * n_steps - 1 - s)

    def phase(s):
        return jnp.where(s < n_steps, 0, 1)

    head_spec = pl.BlockSpec((1, 1, SSD_HEADS, LANES), lambda b, s: (l, phase(s), 0, 0))
    return pl.pallas_call(
        functools.partial(_ssd_body, n_steps=n_steps, ts=ts),
        out_shape=[jax.ShapeDtypeStruct((nb, t, SSD_INNER), BF16),
                   jax.ShapeDtypeStruct((nb, 2, SSD_STATE, SSD_INNER), F32)],
        grid=(nb, 2 * n_steps),
        in_specs=[
            pl.BlockSpec((1, ts, SSD_XBC), lambda b, s: (b, tok_dt(s), 0)),
            pl.BlockSpec((1, ts, LANES), lambda b, s: (b, tok_dt(s), phase(s))),
            pl.BlockSpec((1, ts, SSD_INNER), lambda b, s: (b, tok_out(s), 0)),
            head_spec,
            head_spec,
            _layer_spec(prm["d_full"], l),
            _layer_spec(prm["norm_g"], l),
            pl.BlockSpec((1, 1, SSD_STATE, SSD_INNER), lambda b, s: (b, phase(s), 0, 0)),
        ],
        out_specs=[
            pl.BlockSpec((1, ts, SSD_INNER), lambda b, s: (b, tok_out(s), 0)),
            pl.BlockSpec((1, 1, SSD_STATE, SSD_INNER), lambda b, s: (b, phase(s), 0, 0)),
        ],
        scratch_shapes=[pltpu.VMEM((t, SSD_INNER), F32),
                        pltpu.VMEM((3, (ts // CHUNK) * SSD_HEADS, CHUNK), F32), pltpu.VMEM((ts, LANES), F32)],
        compiler_params=_params(2),
        name="ssd_scan",
    )(xbc, dt, z, prm["a_log"], prm["dt_bias"], prm["d_full"], prm["norm_g"], init)


def _ssd_params(a_log, dt_bias, d_skip, norm_g):
    nl = a_log.shape[0]
    rep = lambda v: jnp.broadcast_to(v[:, :, :, None], (nl, 2, SSD_HEADS, LANES))
    return dict(
        a_log=rep(a_log), dt_bias=rep(dt_bias),
        d_full=jnp.repeat(d_skip, SSD_HEAD_DIM, axis=1).reshape(nl, 1, SSD_INNER),
        norm_g=norm_g.reshape(nl, 1, SSD_INNER))


def _cv_body(x_ref, w_ref, b_ref, lg_ref, lb_ref, o_ref, pad_ref, *, width, n_rows):
    zeros = jnp.zeros((CV_PAD, CONV_DIM), F32)
    pad_ref[0:CV_PAD, :] = zeros
    pad_ref[CV_PAD + width:, :] = zeros
    n_q = (CONV_WIDTH + 8) // 8

    def row_body(r, carry):
        xr = x_ref[r]
        u = xr[:, :CONV_DIM].astype(F32) * _sigmoid(xr[:, CONV_DIM:].astype(F32))
        pad_ref[CV_PAD:CV_PAD + width, :] = u
        cols = []
        for lt in range(CONV_DIM // LANES):
            ls = slice(lt * LANES, (lt + 1) * LANES)
            acc = jnp.broadcast_to(b_ref[0, :, ls], (width, LANES))
            for rr in range(8):
                shifted = pad_ref[rr:rr + width + 8 * (n_q - 1), ls]
                for q in range(n_q):
                    k = 8 * q + rr - (CV_PAD - CONV_WIDTH // 2)
                    if 0 <= k < CONV_WIDTH:
                        acc = acc + shifted[8 * q:8 * q + width] * w_ref[0, k:k + 1, ls]
            cols.append(acc)
        v = jnp.concatenate(cols, axis=1)
        mu = jnp.mean(v, axis=-1, keepdims=True)
        dv = v - mu
        var = jnp.mean(dv * dv, axis=-1, keepdims=True)
        yn = dv * lax.rsqrt(var + EPS) * lg_ref[0] + lb_ref[0]
        o_ref[r] = _silu(yn).astype(o_ref.dtype)
        return carry

    lax.fori_loop(0, n_rows, row_body, 0)


def _conv_branch(cv, width, l, dw_w, dw_b, ln_g, ln_b):
    n = cv.shape[0]
    rows = n // width
    n_rows = max(1, min(rows, 1024 // width))
    x3 = cv.reshape(rows, width, 2 * CONV_DIM)
    out = pl.pallas_call(
        functools.partial(_cv_body, width=width, n_rows=n_rows),
        out_shape=jax.ShapeDtypeStruct((rows, width, CONV_DIM), BF16),
        grid=(rows // n_rows,),
        in_specs=[pl.BlockSpec((n_rows, width, 2 * CONV_DIM), lambda i: (i, 0, 0)),
                  _layer_spec(dw_w, l), _layer_spec(dw_b, l), _layer_spec(ln_g, l), _layer_spec(ln_b, l)],
        out_specs=pl.BlockSpec((n_rows, width, CONV_DIM), lambda i: (i, 0, 0)),
        scratch_shapes=[pltpu.VMEM((width + 2 * CV_PAD, CONV_DIM), F32)],
        compiler_params=_params(1),
        name="conv_branch",
    )(x3, dw_w, dw_b, ln_g, ln_b)
    return out.reshape(n, CONV_DIM)


def _toeplitz_pairs(dw_w):
    half = CONV_WIDTH // 2
    s_idx, t_idx = np.meshgrid(np.arange(LANES), np.arange(LANES), indexing="ij")
    same_row = (s_idx // GRID_W) == (t_idx // GRID_W)
    sel = np.stack([(s_idx - t_idx + half == k) & same_row for k in range(CONV_WIDTH)]).astype(np.float32)
    taps = jnp.swapaxes(dw_w, 1, 2).reshape(-1, CONV_WIDTH).astype(BF16)
    return jnp.einsum("ck,kst->cst", taps, jnp.asarray(sel, BF16), preferred_element_type=BF16)


def _cvmx_body(a_ref, b_ref, t_ref, bias_ref, o_ref, *, n_ch):
    def ch_body(c, carry):
        u = a_ref[c].astype(F32) * _sigmoid(b_ref[c].astype(F32))
        o_ref[c] = (_dot(u.astype(BF16), t_ref[c]) + bias_ref[c]).astype(o_ref.dtype)
        return carry

    lax.fori_loop(0, n_ch, ch_body, 0, unroll=8)


def _conv_branch_mxu(cv_t, l, toe, bias):
    n = cv_t.shape[1]
    sr = n // LANES
    rb = min(256, sr)
    cbk = 64
    x3 = cv_t.reshape(2 * CONV_DIM, sr, LANES)
    nblk = CONV_DIM // cbk
    out = pl.pallas_call(
        functools.partial(_cvmx_body, n_ch=cbk),
        out_shape=jax.ShapeDtypeStruct((CONV_DIM, sr, LANES), BF16),
        grid=(nblk, sr // rb),
        in_specs=[pl.BlockSpec((cbk, rb, LANES), lambda c, i: (c, i, 0)),
                  pl.BlockSpec((cbk, rb, LANES), lambda c, i: (c + nblk, i, 0)),
                  pl.BlockSpec((cbk, LANES, LANES), lambda c, i: (l * nblk + c, 0, 0)),
                  pl.BlockSpec((cbk, 1, LANES), lambda c, i: (l * nblk + c, 0, 0))],
        out_specs=pl.BlockSpec((cbk, rb, LANES), lambda c, i: (c, i, 0)),
        compiler_params=_params(2),
        name="conv_branch_mxu",
    )(x3, x3, toe, bias)
    return out.reshape(CONV_DIM, n)


def _pool_body(x_ref, ic_ref, o_ref, *, n):
    max_lo = max(POOL_WINDOWS) // 2
    for k, w in enumerate(POOL_WINDOWS):
        ls = slice(k * POOL_PAD, (k + 1) * POOL_PAD)
        x = x_ref[0, :, :, ls].astype(F32)
        zeros = jnp.zeros((max_lo,) + x.shape[1:], F32)
        run = jnp.concatenate([zeros, x, zeros], axis=0)
        span = 1
        while span < w:
            run = run[:-span] + run[span:]
            span *= 2
        lo = w // 2
        total = run[max_lo - lo:max_lo - lo + n]
        o_ref[0, :, :, ls] = (total * ic_ref[:, :, ls] - x).astype(o_ref.dtype)


def _pool_branch(pool_in, n, cb):
    nb, _, c, _ = pool_in.shape
    pos = np.arange(n)
    inv = np.zeros((n, 1, POOL_PADDED), np.float32)
    for k, w in enumerate(POOL_WINDOWS):
        lo = w // 2
        cnt = np.clip(pos + (w - 1 - lo) + 1, 0, n) - np.clip(pos - lo, 0, n)
        inv[:, 0, k * POOL_PAD:(k + 1) * POOL_PAD] = (1.0 / cnt)[:, None]
    return pl.pallas_call(
        functools.partial(_pool_body, n=n),
        out_shape=jax.ShapeDtypeStruct(pool_in.shape, BF16),
        grid=(nb, c // cb),
        in_specs=[pl.BlockSpec((1, n, cb, POOL_PADDED), lambda b, j: (b, 0, j, 0)),
                  pl.BlockSpec((n, 1, POOL_PADDED), lambda b, j: (0, 0, 0), pipeline_mode=pl.Buffered(1))],
        out_specs=pl.BlockSpec((1, n, cb, POOL_PADDED), lambda b, j: (b, 0, j, 0)),
        compiler_params=_params(2),
        name="pool_branch",
    )(pool_in, jnp.asarray(inv))


def _pool_seq_body(x_ref, band_ref, ic_ref, o_ref):
    for k in range(POOL_GROUPS):
        ls = slice(k * POOL_PAD, (k + 1) * POOL_PAD)
        x = x_ref[:, ls]
        o_ref[:, ls] = (_dot(band_ref[k], x) * ic_ref[k] - x.astype(F32)).astype(o_ref.dtype)


def _pool_seq(pool_in, n):
    pos = np.arange(n)
    band = np.zeros((POOL_GROUPS, n, n), np.float32)
    inv = np.zeros((POOL_GROUPS, n, POOL_PAD), np.float32)
    for k, w in enumerate(POOL_WINDOWS):
        lo = w // 2
        inside = (pos[None, :] >= pos[:, None] - lo) & (pos[None, :] <= pos[:, None] + (w - 1 - lo))
        band[k] = inside
        inv[k] = (1.0 / inside.sum(axis=1))[:, None]
    whole = lambda shape: pl.BlockSpec(shape, lambda b: (0,) * len(shape), pipeline_mode=pl.Buffered(1))
    return pl.pallas_call(
        _pool_seq_body,
        out_shape=jax.ShapeDtypeStruct(pool_in.shape, BF16),
        grid=(pool_in.shape[0] // n,),
        in_specs=[pl.BlockSpec((n, POOL_PADDED), lambda b: (b, 0)), whole(band.shape), whole(inv.shape)],
        out_specs=pl.BlockSpec((n, POOL_PADDED), lambda b: (b, 0)),
        compiler_params=_params(1),
        name="pool_seq",
    )(pool_in, jnp.asarray(band, BF16), jnp.asarray(inv))


def _merge_body(x_ref, ys_ref, uc_ref, pp_ref, gt_ref, mg_ref, ps_ref, lg_ref, lb_ref, wss_ref, wcv_ref, wpl_ref,
                wo_ref, o_ref, *, cv_transposed):
    d = D_MODEL
    y_ssd = _dot(ys_ref[...], wss_ref[0])
    if cv_transposed:
        v = uc_ref[...].astype(F32)
        mu = jnp.mean(v, axis=0, keepdims=True)
        dv = v - mu
        var = jnp.mean(dv * dv, axis=0, keepdims=True)
        reps = v.shape[1] // LANES
        yn = dv * lax.rsqrt(var + EPS) * jnp.tile(lg_ref[0], (1, reps)) + jnp.tile(lb_ref[0], (1, reps))
        y_cv = lax.dot_general(_silu(yn).astype(BF16), wcv_ref[0], (((0,), (0,)), ((), ())),
                               preferred_element_type=F32)
    else:
        y_cv = _dot(uc_ref[...], wcv_ref[0])
    y_pool = jnp.concatenate(
        [_dot(pp_ref[:, k * POOL_PAD:(k + 1) * POOL_PAD], wpl_ref[0, k]) for k in range(POOL_GROUPS)], axis=1)
    y_pool = y_pool * ps_ref[0]
    merged2 = ((jnp.tanh(gt_ref[:, 0:d].astype(F32)) + 1.0) * y_ssd
               + (jnp.tanh(gt_ref[:, d:2 * d].astype(F32)) + 1.0) * y_cv
               + (jnp.tanh(gt_ref[:, 2 * d:3 * d].astype(F32)) + 1.0) * y_pool)
    o_ref[...] = x_ref[...] + (0.5 * mg_ref[0, 0]) * _dot(merged2.astype(BF16), wo_ref[0])


def _merge(x, ys, uc, pp, gates, l, mods, row_of_tile, prm, tm, cv_transposed):
    n, d = x.shape
    tok = lambda w: pl.BlockSpec((tm, w), lambda i: (i, 0))
    uc_spec = pl.BlockSpec((CONV_DIM, tm), lambda i: (0, i)) if cv_transposed else tok(CONV_DIM)
    names = ("pool_scale", "ln_g_rows", "ln_b_rows", "w_ssd", "w_cv", "w_pool", "w_out")
    return pl.pallas_call(
        functools.partial(_merge_body, cv_transposed=cv_transposed),
        out_shape=jax.ShapeDtypeStruct((n, d), F32),
        grid=(n // tm,),
        in_specs=[tok(d), tok(SSD_INNER), uc_spec, tok(POOL_PADDED), tok(3 * d), _mod_spec(l, 2, row_of_tile)]
                 + [_layer_spec(prm[k], l) for k in names],
        out_specs=tok(d),
        compiler_params=_params(1),
        name="merge",
    )(x, ys, uc, pp, gates, mods, *[prm[k] for k in names])


def _ffn_body(x_ref, g_ref, sh_ref, sc_ref, gt_ref, wg_ref, wu_ref, wd_ref, fg_ref, o_ref, *, final, hid_chunk):
    x = x_ref[...]
    hb = _norm_mod(x, g_ref[0], sh_ref[0, 0], sc_ref[0, 0]).astype(BF16)
    acc = jnp.zeros(x.shape, F32)
    for j in range(0, FFN_HIDDEN, hid_chunk):
        je = min(j + hid_chunk, FFN_HIDDEN)
        gate = _dot(hb, wg_ref[0, :, j:je])
        up = _dot(hb, wu_ref[0, :, j:je])
        acc = acc + _dot((_silu(gate) * up).astype(BF16), wd_ref[0, j:je, :])
    out = x + gt_ref[0, 0] * acc
    if final:
        ms = jnp.mean(out * out, axis=-1, keepdims=True)
        out = (out * lax.rsqrt(ms + EPS)) * fg_ref[...]
    o_ref[...] = out


def _ffn(x, l, norm_g, mods, row_of_tile, w_gate, w_up, w_down, final_g, final, tm):
    n, d = x.shape
    return pl.pallas_call(
        functools.partial(_ffn_body, final=final, hid_chunk=6 * MXU_DIM),
        out_shape=jax.ShapeDtypeStruct((n, d), F32),
        grid=(n // tm,),
        in_specs=[pl.BlockSpec((tm, d), lambda i: (i, 0)), _layer_spec(norm_g, l),
                  _mod_spec(l, 3, row_of_tile), _mod_spec(l, 4, row_of_tile), _mod_spec(l, 5, row_of_tile),
                  _layer_spec(w_gate, l), _layer_spec(w_up, l), _layer_spec(w_down, l),
                  pl.BlockSpec((1, d), lambda i: (0, 0), pipeline_mode=pl.Buffered(1))],
        out_specs=pl.BlockSpec((tm, d), lambda i: (i, 0)),
        compiler_params=_params(1),
        name="ffn",
    )(x, norm_g, mods, mods, mods, w_gate, w_up, w_down, final_g.reshape(1, d))


W_MAIN_COLS = dict(z=(0, SSD_INNER), xbc=(SSD_INNER, SSD_XBC), dt=(SSD_INNER + SSD_XBC, DT_PAD),
                   pool=(SSD_INNER + SSD_XBC + DT_PAD, POOL_PADDED),
                   gates=(SSD_INNER + SSD_XBC + DT_PAD + POOL_PADDED, 3 * D_MODEL))


def _split_w_body(w_ref, main_ref, cv_ref, cvt_ref):
    o = [int(v) for v in np.cumsum([0, SSD_INNER, SSD_XBC, 2 * SSD_HEADS, 2 * CONV_DIM, POOL_DIM, 3 * D_MODEL])]
    rb = w_ref.shape[2]
    rows = lambda a, b: w_ref[0, a:b, :]

    def padded(a, b, n):
        return jnp.concatenate([rows(a, b), jnp.zeros((n - (b - a), rb), F32)], axis=0)

    parts = [rows(o[0], o[2])]
    for k in range(2):
        parts.append(padded(o[2] + k * SSD_HEADS, o[2] + (k + 1) * SSD_HEADS, LANES))
    for k in range(POOL_GROUPS):
        parts.append(padded(o[4] + k * POOL_GROUP_DIM, o[4] + (k + 1) * POOL_GROUP_DIM, POOL_PAD))
    parts.append(0.5 * rows(o[5], o[6]))
    off = 0
    for p in parts:
        main_ref[0, :, off:off + p.shape[0]] = p.T.astype(BF16)
        off += p.shape[0]
    cv_t = rows(o[3], o[4])
    cvt_ref[0] = cv_t.astype(BF16)
    cv_ref[0] = cv_t.T.astype(BF16)


def _split_w_in(w):
    nl, d, n_in = w.shape
    rb = 256
    n_main = sum(c for _, c in W_MAIN_COLS.values())
    return pl.pallas_call(
        _split_w_body,
        out_shape=[jax.ShapeDtypeStruct((nl, d, n_main), BF16), jax.ShapeDtypeStruct((nl, d, 2 * CONV_DIM), BF16),
                   jax.ShapeDtypeStruct((nl, 2 * CONV_DIM, d), BF16)],
        grid=(nl, d // rb),
        in_specs=[pl.BlockSpec((1, n_in, rb), lambda l, i: (l, 0, i))],
        out_specs=[pl.BlockSpec((1, rb, n_main), lambda l, i: (l, i, 0)),
                   pl.BlockSpec((1, rb, 2 * CONV_DIM), lambda l, i: (l, i, 0)),
                   pl.BlockSpec((1, 2 * CONV_DIM, rb), lambda l, i: (l, 0, i))],
        compiler_params=_params(2),
        name="split_w_in",
    )(jnp.swapaxes(w, 1, 2))


def kernel(x, c, ctx, c_ctx, w_ada, b_ada, norm1_g, norm2_g, w_in, ssd_conv_w, ssd_conv_b, ssd_a_log,
           ssd_dt_bias, ssd_d, ssd_norm_g, ssd_w_out, cv_dw_w, cv_dw_b, cv_ln_g, cv_ln_b, cv_w_out, pool_w,
           pool_scale, w_out, ffn_w_gate, ffn_w_up, ffn_w_down, final_g):
    nb, t, d = x.shape
    tc = ctx.shape[1]
    depth = w_in.shape[0]
    rows = t // GRID_W

    n_cond = -(-(nb + 1) // 8) * 8
    cc = jnp.concatenate([c, c_ctx[None, :], jnp.zeros((n_cond - nb - 1, d), F32)], axis=0)
    mods = _ada(cc, w_ada, b_ada)

    tm_lat = min(512, t)
    tm_ctx = min(256, tc)
    ts_lat = min(1024, t)
    ts_ctx = min(256, tc)
    cb_lat = min(16, GRID_W)
    lat_row = lambda i: i // (t // tm_lat)
    ctx_row = lambda i: nb

    w_main, w_cv, w_cv_t = _split_w_in(w_in)
    seg = lambda name, arr=0, tr=False: (arr,) + W_MAIN_COLS[name] + (tr,)
    w_fg, w_fu, w_fd = (v.astype(BF16) for v in (ffn_w_gate, ffn_w_up, ffn_w_down))
    row3 = lambda v: v.reshape(depth, 1, v.shape[-1])
    norm1, norm2 = row3(norm1_g), row3(norm2_g)
    ln_rows = lambda v: jnp.broadcast_to(v[:, :, None], (depth, CONV_DIM, LANES))
    merge_prm = dict(
        pool_scale=row3(pool_scale), ln_g_rows=ln_rows(cv_ln_g), ln_b_rows=ln_rows(cv_ln_b),
        w_ssd=ssd_w_out.astype(BF16), w_cv=cv_w_out.astype(BF16),
        w_pool=jnp.pad(pool_w, ((0, 0), (0, 0), (0, POOL_PAD - POOL_GROUP_DIM), (0, 0))).astype(BF16),
        w_out=w_out.astype(BF16))
    ssd_prm = _ssd_params(ssd_a_log, ssd_dt_bias, ssd_d, ssd_norm_g)
    conv5 = (jnp.pad(ssd_conv_w, ((0, 0), (0, 8 - SSD_CONV), (0, 0))), row3(ssd_conv_b))
    cv_vpu_prm = (jnp.pad(cv_dw_w, ((0, 0), (0, 32 - CONV_WIDTH), (0, 0))), row3(cv_dw_b), row3(cv_ln_g), row3(cv_ln_b))
    toe = _toeplitz_pairs(cv_dw_w)
    cv_bias = jnp.broadcast_to(cv_dw_b.reshape(-1)[:, None, None], (depth * CONV_DIM, 1, LANES))

    xl = x.reshape(nb * t, d)
    xc = ctx.reshape(nb * tc, d)
    zero_state = jnp.zeros((nb, 2, SSD_STATE, SSD_INNER), F32)

    for l in range(depth):
        last = l == depth - 1

        if last:
            xbc_c, dt_c = _inproj(xc, l, norm1, mods, ctx_row, [w_main], [seg("xbc"), seg("dt")], [BF16, F32],
                                  tm_ctx, tc, 0, *conv5)
            z_c = jnp.zeros((nb, tc, SSD_INNER), BF16)
        else:
            z_c, xbc_c, dt_c, cv_c, pool_c, gates_c = _inproj(
                xc, l, norm1, mods, ctx_row, [w_main, w_cv],
                [seg("z"), seg("xbc"), seg("dt"), (1, 0, 2 * CONV_DIM, False), seg("pool"), seg("gates")],
                [BF16, BF16, F32, BF16, BF16, BF16], tm_ctx, tc, 1, *conv5)
            z_c = z_c.reshape(nb, tc, SSD_INNER)
        y_c, ctx_state = _ssd(xbc_c.reshape(nb, tc, SSD_XBC), dt_c.reshape(nb, tc, DT_PAD), z_c, l, ssd_prm,
                              zero_state, ts_ctx)
        if not last:
            uc_c = _conv_branch(cv_c, tc, l, *cv_vpu_prm)
            pp_c = _pool_seq(pool_c, tc)
            xc = _merge(xc, y_c.reshape(nb * tc, SSD_INNER), uc_c, pp_c, gates_c, l, mods, ctx_row, merge_prm,
                        tm_ctx, False)
            xc = _ffn(xc, l, norm2, mods, ctx_row, w_fg, w_fu, w_fd, final_g, False, tm_ctx)

        z_l, xbc_l, dt_l, cv_l, pool_l, gates_l = _inproj(
            xl, l, norm1, mods, lat_row, [w_main, w_cv_t],
            [seg("z"), seg("xbc"), seg("dt"), (1, 0, 2 * CONV_DIM, True), seg("pool"), seg("gates")],
            [BF16, BF16, F32, BF16, BF16, BF16], tm_lat, t, 1, *conv5)
        y_l, _ = _ssd(xbc_l.reshape(nb, t, SSD_XBC), dt_l.reshape(nb, t, DT_PAD), z_l.reshape(nb, t, SSD_INNER),
                      l, ssd_prm, ctx_state, ts_lat)
        v_l = _conv_branch_mxu(cv_l, l, toe, cv_bias)
        pp_l = _pool_branch(pool_l.reshape(nb, rows, GRID_W, POOL_PADDED), rows, cb_lat).reshape(-1, POOL_PADDED)
        xl = _merge(xl, y_l.reshape(nb * t, SSD_INNER), v_l, pp_l, gates_l, l, mods, lat_row, merge_prm,
                    tm_lat, True)
        xl = _ffn(xl, l, norm2, mods, lat_row, w_fg, w_fu, w_fd, final_g, last, tm_lat)

    return xl.reshape(nb, t, d)
```

```python
import functools

import numpy as np
import jax
import jax.numpy as jnp
from jax import lax
from jax.experimental import pallas as pl
from jax.experimental.pallas import tpu as pltpu

F32 = jnp.float32
BF16 = jnp.bfloat16

D_MODEL = 1024
GRID_W = 64
EPS = 1e-6
N_MOD = 6
SSD_HEADS = 16
SSD_HEAD_DIM = 64
SSD_INNER = SSD_HEADS * SSD_HEAD_DIM
SSD_GROUPS = 2
SSD_STATE = 128
SSD_CONV = 5
SSD_XBC = SSD_INNER + 2 * SSD_GROUPS * SSD_STATE
CONV_DIM = 768
CONV_WIDTH = 31
POOL_DIM = 768
POOL_WINDOWS = (2, 4, 8, 16)
POOL_GROUPS = len(POOL_WINDOWS)
POOL_GROUP_DIM = POOL_DIM // POOL_GROUPS
POOL_OUT_DIM = D_MODEL // POOL_GROUPS
FFN_HIDDEN = 2816

LANES = 128
MXU_DIM = 256
CHUNK = 128
LOG2E = 1.4426950408889634
POOL_PAD = 256
POOL_PADDED = POOL_GROUPS * POOL_PAD
DT_PAD = 2 * LANES
HALO = 16
CV_PAD = 16
VMEM_LIMIT = 56 * 1024 * 1024


def _layer_spec(arr, l):
    shape = arr.shape[1:]
    nd = len(shape)
    return pl.BlockSpec((1,) + shape, lambda *_: (l,) + (0,) * nd, pipeline_mode=pl.Buffered(1))


def _mod_spec(l, k, row_of_tile):
    return pl.BlockSpec((1, 1, 1, D_MODEL), lambda i: (l, row_of_tile(i), 0, k))


def _params(n_axes):
    return pltpu.CompilerParams(dimension_semantics=("arbitrary",) * n_axes, vmem_limit_bytes=VMEM_LIMIT)


def _sigmoid(v):
    return 0.5 * jnp.tanh(0.5 * v) + 0.5


def _silu(v):
    return v * _sigmoid(v)


def _dot(a, b):
    return jnp.dot(a, b, preferred_element_type=F32)


def _dot_nt(a, b):
    return lax.dot_general(a, b, (((1,), (1,)), ((), ())), preferred_element_type=F32)


def _ada_body(c_ref, w_ref, b_ref, o_ref):
    s = _silu(c_ref[...])
    o_ref[0] = jnp.dot(s, w_ref[0], preferred_element_type=F32,
                       precision=lax.Precision.HIGHEST) + b_ref[0]


def _ada(cc, w_ada, b_ada):
    nl, d, nm = w_ada.shape
    r = cc.shape[0]
    tn = 1024
    return pl.pallas_call(
        _ada_body,
        out_shape=jax.ShapeDtypeStruct((nl, r, nm), F32),
        grid=(nl, nm // tn),
        in_specs=[pl.BlockSpec((r, d), lambda l, j: (0, 0)),
                  pl.BlockSpec((1, d, tn), lambda l, j: (l, 0, j)),
                  pl.BlockSpec((1, 1, tn), lambda l, j: (l, 0, j))],
        out_specs=pl.BlockSpec((1, r, tn), lambda l, j: (l, 0, j)),
        compiler_params=_params(2),
        name="adaln",
    )(cc, w_ada, b_ada.reshape(nl, 1, nm)).reshape(nl, r, 1, nm)


def _norm_mod(x, g, shift, scale):
    ms = jnp.mean(x * x, axis=-1, keepdims=True)
    h = (x * lax.rsqrt(ms + EPS)) * g
    return h * (1.0 + scale) + shift


def _inproj_body(*refs, n_arr, segs, col_chunk, conv_seg, tiles_per_seq):
    x_ref, xp_ref, xn_ref, g_ref, sh_ref, sc_ref, cw_ref, cb_ref = refs[:8]
    w_refs = refs[8:8 + n_arr]
    o_refs = refs[8 + n_arr:]
    tm = x_ref.shape[0]
    norm = lambda v: _norm_mod(v, g_ref[0], sh_ref[0, 0], sc_ref[0, 0]).astype(BF16)
    hb = norm(x_ref[...])
    pos = pl.program_id(0) % tiles_per_seq
    h_prev = jnp.where(pos > 0, norm(xp_ref[...]), jnp.zeros((HALO, x_ref.shape[1]), BF16))
    h_next = jnp.where(pos < tiles_per_seq - 1, norm(xn_ref[...]), jnp.zeros((HALO, x_ref.shape[1]), BF16))
    hb_ext = jnp.concatenate([h_prev, hb, h_next], axis=0)
    for seg, ((arr, off, n, tr), o_ref) in enumerate(zip(segs, o_refs)):
        w_ref = w_refs[arr]
        step = min(col_chunk, n)
        for j in range(0, n, step):
            if tr:
                o_ref[j:j + step, :] = _dot_nt(w_ref[0, off + j:off + j + step, :], hb).astype(o_ref.dtype)
            elif seg == conv_seg:
                r = _dot(hb_ext, w_ref[0, :, off + j:off + j + step])
                rows = tm + 2 * HALO
                acc = jnp.broadcast_to(cb_ref[0, :, j:j + step], (tm, step))
                for k in range(SSD_CONV):
                    sh = (SSD_CONV // 2 - k) % rows
                    rk = r if sh == 0 else pltpu.roll(r, sh, axis=0)
                    acc = acc + rk[HALO:HALO + tm] * cw_ref[0, k:k + 1, j:j + step]
                hv = 0.5 * acc
                o_ref[:, j:j + step] = (hv + hv * jnp.tanh(hv)).astype(o_ref.dtype)
            else:
                o_ref[:, j:j + step] = _dot(hb, w_ref[0, :, off + j:off + j + step]).astype(o_ref.dtype)


def _inproj(x, l, norm_g, mods, row_of_tile, arrays, segs, out_dtypes, tm, seq_len, conv_seg, conv_w, conv_b):
    n, d = x.shape
    out_shape, out_specs = [], []
    for (_, _, cols, tr), dt in zip(segs, out_dtypes):
        if tr:
            out_shape.append(jax.ShapeDtypeStruct((cols, n), dt))
            out_specs.append(pl.BlockSpec((cols, tm), lambda i: (0, i)))
        else:
            out_shape.append(jax.ShapeDtypeStruct((n, cols), dt))
            out_specs.append(pl.BlockSpec((tm, cols), lambda i: (i, 0)))
    hb = tm // HALO
    return pl.pallas_call(
        functools.partial(_inproj_body, n_arr=len(arrays), segs=tuple(segs), col_chunk=512, conv_seg=conv_seg,
                          tiles_per_seq=seq_len // tm),
        out_shape=out_shape,
        grid=(n // tm,),
        in_specs=[pl.BlockSpec((tm, d), lambda i: (i, 0)),
                  pl.BlockSpec((HALO, d), lambda i: (jnp.maximum(i * hb - 1, 0), 0)),
                  pl.BlockSpec((HALO, d), lambda i: (jnp.minimum((i + 1) * hb, n // HALO - 1), 0)),
                  _layer_spec(norm_g, l), _mod_spec(l, 0, row_of_tile), _mod_spec(l, 1, row_of_tile),
                  _layer_spec(conv_w, l), _layer_spec(conv_b, l)]
                 + [_layer_spec(w, l) for w in arrays],
        out_specs=out_specs,
        compiler_params=_params(1),
        name="inproj",
    )(x, x, x, norm_g, mods, mods, conv_w, conv_b, *arrays)


def _split3(v):
    hi = v.astype(BF16)
    r1 = v - hi.astype(F32)
    mid = r1.astype(BF16)
    lo = (r1 - mid.astype(F32)).astype(BF16)
    return hi, mid, lo


def _ssd_body(xm_ref, dt_ref, z_ref, alog_ref, dtb_ref, dsk_ref,
              ng_ref, init_ref, y_ref, st_ref, yf_all, pt_ref, ec_ref, *, n_steps, ts):
    s = pl.program_id(1)
    fwd = s < n_steps
    c = jnp.where(fwd, s, 2 * n_steps - 1 - s)
    base = pl.multiple_of(c * ts, ts)
    n_sub = ts // CHUNK
    nh = SSD_HEADS
    inner = SSD_INNER
    half_grp = SSD_GROUPS * SSD_STATE
    pairs_per_group = SSD_HEADS // SSD_GROUPS // 2

    @pl.when(jnp.logical_or(s == 0, s == n_steps))
    def _():
        st_ref[...] = init_ref[...]

    @pl.when(fwd)
    def _():
        yf_all[pl.ds(base, ts), :] = xm_ref[0, :, :inner].astype(F32) * dsk_ref[0]

    li = lax.broadcasted_iota(jnp.int32, (CHUNK, CHUNK), 0)
    si = lax.broadcasted_iota(jnp.int32, (CHUNK, CHUNK), 1)
    tri = jnp.where(fwd, li - si, si - li) >= 0
    first_half = si < SSD_HEAD_DIM
    first_half_row = first_half[0:1]

    dtr = dt_ref[0]
    dtr_t = jnp.concatenate([dtr[j * CHUNK:(j + 1) * CHUNK].T[:nh] for j in range(n_sub)], axis=0)
    dtr_t = dtr_t + jnp.tile(dtb_ref[0, 0], (n_sub, 1))
    dt_t = jnp.maximum(dtr_t, 0.0) + jnp.log(1.0 + jnp.exp(-jnp.abs(dtr_t)))
    a2_t = dt_t * jnp.tile(-jnp.exp(alog_ref[0, 0]) * LOG2E, (n_sub, 1))
    order = jnp.where(tri, 1.0, 0.0).astype(BF16)
    pieces = _split3(a2_t)
    e2_t = sum(_dot_nt(p, order) for p in pieces)
    tot2 = jnp.sum(a2_t, axis=1, keepdims=True)
    er2_t = e2_t - jnp.log(dt_t) * LOG2E
    pt_ref[0] = er2_t
    pt_ref[1] = jnp.exp2(tot2 - er2_t)
    pt_ref[2] = jnp.broadcast_to(jnp.exp2(tot2), e2_t.shape)
    zpad = jnp.zeros((CHUNK - nh, CHUNK), BF16)
    for j in range(n_sub):
        ec_ref[j * CHUNK:(j + 1) * CHUNK, :] = sum(
            _dot_nt(order, jnp.concatenate([p[j * nh:(j + 1) * nh], zpad], axis=0)) for p in pieces)

    for j in range(n_sub):
        jj = jnp.where(fwd, j, n_sub - 1 - j)
        loc = pl.multiple_of(jj * CHUNK, CHUNK)
        off = pl.multiple_of(base + loc, CHUNK)
        hrow = pl.multiple_of(jj * nh, nh)
        xc = xm_ref[0, pl.ds(loc, CHUNK), :]
        er2 = pt_ref[0, pl.ds(hrow, nh), :]
        w_t = pt_ref[1, pl.ds(hrow, nh), :]
        etot = pt_ref[2, pl.ds(hrow, nh), :]
        e_col = ec_ref[pl.ds(loc, CHUNK), :]
        y_pairs = []
        for g in range(SSD_GROUPS):
            b_g = xc[:, inner + g * SSD_STATE: inner + (g + 1) * SSD_STATE]
            c_g = xc[:, inner + half_grp + g * SSD_STATE: inner + half_grp + (g + 1) * SSD_STATE]
            cb = _dot_nt(c_g, b_g).astype(BF16)
            b_t = b_g.astype(F32).T.astype(BF16)
            gcols = slice(g * inner // SSD_GROUPS, (g + 1) * inner // SSD_GROUPS)
            y_carry = _dot(c_g, st_ref[0, 0, :, gcols].astype(BF16))
            for pr in range(pairs_per_group):
                pair = g * pairs_per_group + pr
                cols = slice(pair * LANES, (pair + 1) * LANES)
                xp = xc[:, cols]
                zero = jnp.zeros_like(xp)
                x12 = jnp.concatenate([jnp.where(first_half, xp, zero), jnp.where(first_half, zero, xp)], axis=0)
                ms, bw, dec = [], [], []
                for h in (2 * pair, 2 * pair + 1):
                    ecol = jnp.broadcast_to(e_col[:, h:h + 1], (CHUNK, CHUNK))
                    ms.append(jnp.exp2(jnp.where(tri, ecol - er2[h:h + 1, :], -1e30)).astype(BF16) * cb)
                    dec.append(jnp.exp2(ecol))
                    w_rows = jnp.broadcast_to(w_t[h:h + 1, :], (2 * 8, CHUNK)).astype(BF16)
                    bw.append(b_t * jnp.tile(w_rows, (SSD_STATE // (2 * 8), 1)))
                both = _dot(jnp.concatenate([jnp.concatenate(ms, axis=1), jnp.concatenate(bw, axis=1)], axis=0), x12)
                y_pairs.append(both[:CHUNK] + jnp.where(first_half, dec[0], dec[1]) * y_carry[:, pr * LANES:(pr + 1) * LANES])
                erow = jnp.where(first_half_row, etot[2 * pair:2 * pair + 1, :], etot[2 * pair + 1:2 * pair + 2, :])
                st_ref[0, 0, :, cols] = st_ref[0, 0, :, cols] * erow + both[CHUNK:]
        yf_all[pl.ds(off, CHUNK), :] = yf_all[pl.ds(off, CHUNK), :] + jnp.concatenate(y_pairs, axis=1)

    @pl.when(jnp.logical_not(fwd))
    def _():
        y = yf_all[pl.ds(base, ts), :] * _silu(z_ref[0].astype(F32))
        gw = inner // SSD_GROUPS
        for g in range(SSD_GROUPS):
            yg = y[:, g * gw:(g + 1) * gw]
            ms = jnp.mean(yg * yg, axis=-1, keepdims=True)
            yn = (yg * lax.rsqrt(ms + EPS)) * ng_ref[0, :, g * gw:(g + 1) * gw]
            y_ref[0, :, g * gw:(g + 1) * gw] = yn.astype(y_ref.dtype)


def _ssd(xbc, dt, z, l, prm, init, ts):
    nb, t, _ = xbc.shape
    n_steps = t // ts

    def tok_out(s):
        return jnp.where(s < n_steps, n_steps - 1, 2 * n_steps - 1 - s)

    def tok_dt(s):
        return jnp.where(s < n_steps, s, 2 * n_steps - 1 - s)

    def phase(s):
        return jnp.where(s < n_steps, 0, 1)

    head_spec = pl.BlockSpec((1, 1, SSD_HEADS, LANES), lambda b, s: (l, phase(s), 0, 0))
    return pl.pallas_call(
        functools.partial(_ssd_body, n_steps=n_steps, ts=ts),
        out_shape=[jax.ShapeDtypeStruct((nb, t, SSD_INNER), BF16),
                   jax.ShapeDtypeStruct((nb, 2, SSD_STATE, SSD_INNER), F32)],
        grid=(nb, 2 * n_steps),
        in_specs=[
            pl.BlockSpec((1, ts, SSD_XBC), lambda b, s: (b, tok_dt(s), 0)),
            pl.BlockSpec((1, ts, LANES), lambda b, s: (b, tok_dt(s), phase(s))),
            pl.BlockSpec((1, ts, SSD_INNER), lambda b, s: (b, tok_out(s), 0)),
            head_spec,
            head_spec,
            _layer_spec(prm["d_full"], l),
            _layer_spec(prm["norm_g"], l),
            pl.BlockSpec((1, 1, SSD_STATE, SSD_INNER), lambda b, s: (b, phase(s), 0, 0)),
        ],
        out_specs=[
            pl.BlockSpec((1, ts, SSD_INNER), lambda b, s: (b, tok_out(s), 0)),
            pl.BlockSpec((1, 1, SSD_STATE, SSD_INNER), lambda b, s: (b, phase(s), 0, 0)),
        ],
        scratch_shapes=[pltpu.VMEM((t, SSD_INNER), F32),
                        pltpu.VMEM((3, (ts // CHUNK) * SSD_HEADS, CHUNK), F32), pltpu.VMEM((ts, LANES), F32)],
        compiler_params=_params(2),
        name="ssd_scan",
    )(xbc, dt, z, prm["a_log"], prm["dt_bias"], prm["d_full"], prm["norm_g"], init)


def _ssd_params(a_log, dt_bias, d_skip, norm_g):
    nl = a_log.shape[0]
    rep = lambda v: jnp.broadcast_to(v[:, :, :, None], (nl, 2, SSD_HEADS, LANES))
    return dict(
        a_log=rep(a_log), dt_bias=rep(dt_bias),
        d_full=jnp.repeat(d_skip, SSD_HEAD_DIM, axis=1).reshape(nl, 1, SSD_INNER),
        norm_g=norm_g.reshape(nl, 1, SSD_INNER))


def _cv_body(x_ref, w_ref, b_ref, lg_ref, lb_ref, o_ref, pad_ref, *, width, n_rows):
    zeros = jnp.zeros((CV_PAD, CONV_DIM), F32)
    pad_ref[0:CV_PAD, :] = zeros
    pad_ref[CV_PAD + width:, :] = zeros
    n_q = (CONV_WIDTH + 8) // 8

    def row_body(r, carry):
        xr = x_ref[r]
        u = xr[:, :CONV_DIM].astype(F32) * _sigmoid(xr[:, CONV_DIM:].astype(F32))
        pad_ref[CV_PAD:CV_PAD + width, :] = u
        cols = []
        for lt in range(CONV_DIM // LANES):
            ls = slice(lt * LANES, (lt + 1) * LANES)
            acc = jnp.broadcast_to(b_ref[0, :, ls], (width, LANES))
            for rr in range(8):
                shifted = pad_ref[rr:rr + width + 8 * (n_q - 1), ls]
                for q in range(n_q):
                    k = 8 * q + rr - (CV_PAD - CONV_WIDTH // 2)
                    if 0 <= k < CONV_WIDTH:
                        acc = acc + shifted[8 * q:8 * q + width] * w_ref[0, k:k + 1, ls]
            cols.append(acc)
        v = jnp.concatenate(cols, axis=1)
        mu = jnp.mean(v, axis=-1, keepdims=True)
        dv = v - mu
        var = jnp.mean(dv * dv, axis=-1, keepdims=True)
        yn = dv * lax.rsqrt(var + EPS) * lg_ref[0] + lb_ref[0]
        o_ref[r] = _silu(yn).astype(o_ref.dtype)
        return carry

    lax.fori_loop(0, n_rows, row_body, 0)


def _conv_branch(cv, width, l, dw_w, dw_b, ln_g, ln_b):
    n = cv.shape[0]
    rows = n // width
    n_rows = max(1, min(rows, 1024 // width))
    x3 = cv.reshape(rows, width, 2 * CONV_DIM)
    out = pl.pallas_call(
        functools.partial(_cv_body, width=width, n_rows=n_rows),
        out_shape=jax.ShapeDtypeStruct((rows, width, CONV_DIM), BF16),
        grid=(rows // n_rows,),
        in_specs=[pl.BlockSpec((n_rows, width, 2 * CONV_DIM), lambda i: (i, 0, 0)),
                  _layer_spec(dw_w, l), _layer_spec(dw_b, l), _layer_spec(ln_g, l), _layer_spec(ln_b, l)],
        out_specs=pl.BlockSpec((n_rows, width, CONV_DIM), lambda i: (i, 0, 0)),
        scratch_shapes=[pltpu.VMEM((width + 2 * CV_PAD, CONV_DIM), F32)],
        compiler_params=_params(1),
        name="conv_branch",
    )(x3, dw_w, dw_b, ln_g, ln_b)
    return out.reshape(n, CONV_DIM)


def _toeplitz_pairs(dw_w):
    half = CONV_WIDTH // 2
    s_idx, t_idx = np.meshgrid(np.arange(LANES), np.arange(LANES), indexing="ij")
    same_row = (s_idx // GRID_W) == (t_idx // GRID_W)
    sel = np.stack([(s_idx - t_idx + half == k) & same_row for k in range(CONV_WIDTH)]).astype(np.float32)
    taps = jnp.swapaxes(dw_w, 1, 2).reshape(-1, CONV_WIDTH).astype(BF16)
    return jnp.einsum("ck,kst->cst", taps, jnp.asarray(sel, BF16), preferred_element_type=BF16)


def _cvmx_body(a_ref, b_ref, t_ref, bias_ref, o_ref, *, n_ch):
    def ch_body(c, carry):
        u = a_ref[c].astype(F32) * _sigmoid(b_ref[c].astype(F32))
        o_ref[c] = (_dot(u.astype(BF16), t_ref[c]) + bias_ref[c]).astype(o_ref.dtype)
        return carry

    lax.fori_loop(0, n_ch, ch_body, 0, unroll=8)


def _conv_branch_mxu(cv_t, l, toe, bias):
    n = cv_t.shape[1]
    sr = n // LANES
    rb = min(256, sr)
    cbk = 64
    x3 = cv_t.reshape(2 * CONV_DIM, sr, LANES)
    nblk = CONV_DIM // cbk
    out = pl.pallas_call(
        functools.partial(_cvmx_body, n_ch=cbk),
        out_shape=jax.ShapeDtypeStruct((CONV_DIM, sr, LANES), BF16),
        grid=(nblk, sr // rb),
        in_specs=[pl.BlockSpec((cbk, rb, LANES), lambda c, i: (c, i, 0)),
                  pl.BlockSpec((cbk, rb, LANES), lambda c, i: (c + nblk, i, 0)),
                  pl.BlockSpec((cbk, LANES, LANES), lambda c, i: (l * nblk + c, 0, 0)),
                  pl.BlockSpec((cbk, 1, LANES), lambda c, i: (l * nblk + c, 0, 0))],
        out_specs=pl.BlockSpec((cbk, rb, LANES), lambda c, i: (c, i, 0)),
        compiler_params=_params(2),
        name="conv_branch_mxu",
    )(x3, x3, toe, bias)
    return out.reshape(CONV_DIM, n)


def _pool_body(x_ref, ic_ref, o_ref, *, n):
    max_lo = max(POOL_WINDOWS) // 2
    for k, w in enumerate(POOL_WINDOWS):
        ls = slice(k * POOL_PAD, (k + 1) * POOL_PAD)
        x = x_ref[0, :, :, ls].astype(F32)
        zeros = jnp.zeros((max_lo,) + x.shape[1:], F32)
        run = jnp.concatenate([zeros, x, zeros], axis=0)
        span = 1
        while span < w:
            run = run[:-span] + run[span:]
            span *= 2
        lo = w // 2
        total = run[max_lo - lo:max_lo - lo + n]
        o_ref[0, :, :, ls] = (total * ic_ref[:, :, ls] - x).astype(o_ref.dtype)


def _pool_branch(pool_in, n, cb):
    nb, _, c, _ = pool_in.shape
    pos = np.arange(n)
    inv = np.zeros((n, 1, POOL_PADDED), np.float32)
    for k, w in enumerate(POOL_WINDOWS):
        lo = w // 2
        cnt = np.clip(pos + (w - 1 - lo) + 1, 0, n) - np.clip(pos - lo, 0, n)
        inv[:, 0, k * POOL_PAD:(k + 1) * POOL_PAD] = (1.0 / cnt)[:, None]
    return pl.pallas_call(
        functools.partial(_pool_body, n=n),
        out_shape=jax.ShapeDtypeStruct(pool_in.shape, BF16),
        grid=(nb, c // cb),
        in_specs=[pl.BlockSpec((1, n, cb, POOL_PADDED), lambda b, j: (b, 0, j, 0)),
                  pl.BlockSpec((n, 1, POOL_PADDED), lambda b, j: (0, 0, 0), pipeline_mode=pl.Buffered(1))],
        out_specs=pl.BlockSpec((1, n, cb, POOL_PADDED), lambda b, j: (b, 0, j, 0)),
        compiler_params=_params(2),
        name="pool_branch",
    )(pool_in, jnp.asarray(inv))


def _pool_seq_body(x_ref, band_ref, ic_ref, o_ref):
    for k in range(POOL_GROUPS):
        ls = slice(k * POOL_PAD, (k + 1) * POOL_PAD)
        x = x_ref[:, ls]
        o_ref[:, ls] = (_dot(band_ref[k], x) * ic_ref[k] - x.astype(F32)).astype(o_ref.dtype)


def _pool_seq(pool_in, n):
    pos = np.arange(n)
    band = np.zeros((POOL_GROUPS, n, n), np.float32)
    inv = np.zeros((POOL_GROUPS, n, POOL_PAD), np.float32)
    for k, w in enumerate(POOL_WINDOWS):
        lo = w // 2
        inside = (pos[None, :] >= pos[:, None] - lo) & (pos[None, :] <= pos[:, None] + (w - 1 - lo))
        band[k] = inside
        inv[k] = (1.0 / inside.sum(axis=1))[:, None]
    whole = lambda shape: pl.BlockSpec(shape, lambda b: (0,) * len(shape), pipeline_mode=pl.Buffered(1))
    return pl.pallas_call(
        _pool_seq_body,
        out_shape=jax.ShapeDtypeStruct(pool_in.shape, BF16),
        grid=(pool_in.shape[0] // n,),
        in_specs=[pl.BlockSpec((n, POOL_PADDED), lambda b: (b, 0)), whole(band.shape), whole(inv.shape)],
        out_specs=pl.BlockSpec((n, POOL_PADDED), lambda b: (b, 0)),
        compiler_params=_params(1),
        name="pool_seq",
    )(pool_in, jnp.asarray(band, BF16), jnp.asarray(inv))


def _merge_body(x_ref, ys_ref, uc_ref, pp_ref, gt_ref, mg_ref, ps_ref, lg_ref, lb_ref, wss_ref, wcv_ref, wpl_ref,
                wo_ref, o_ref, *, cv_transposed):
    d = D_MODEL
    y_ssd = _dot(ys_ref[...], wss_ref[0])
    if cv_transposed:
        v = uc_ref[...].astype(F32)
        mu = jnp.mean(v, axis=0, keepdims=True)
        dv = v - mu
        var = jnp.mean(dv * dv, axis=0, keepdims=True)
        reps = v.shape[1] // LANES
        yn = dv * lax.rsqrt(var + EPS) * jnp.tile(lg_ref[0], (1, reps)) + jnp.tile(lb_ref[0], (1, reps))
        y_cv = lax.dot_general(_silu(yn).astype(BF16), wcv_ref[0], (((0,), (0,)), ((), ())),
                               preferred_element_type=F32)
    else:
        y_cv = _dot(uc_ref[...], wcv_ref[0])
    y_pool = jnp.concatenate(
        [_dot(pp_ref[:, k * POOL_PAD:(k + 1) * POOL_PAD], wpl_ref[0, k]) for k in range(POOL_GROUPS)], axis=1)
    y_pool = y_pool * ps_ref[0]
    merged2 = ((jnp.tanh(gt_ref[:, 0:d].astype(F32)) + 1.0) * y_ssd
               + (jnp.tanh(gt_ref[:, d:2 * d].astype(F32)) + 1.0) * y_cv
               + (jnp.tanh(gt_ref[:, 2 * d:3 * d].astype(F32)) + 1.0) * y_pool)
    o_ref[...] = x_ref[...] + (0.5 * mg_ref[0, 0]) * _dot(merged2.astype(BF16), wo_ref[0])


def _merge(x, ys, uc, pp, gates, l, mods, row_of_tile, prm, tm, cv_transposed):
    n, d = x.shape
    tok = lambda w: pl.BlockSpec((tm, w), lambda i: (i, 0))
    uc_spec = pl.BlockSpec((CONV_DIM, tm), lambda i: (0, i)) if cv_transposed else tok(CONV_DIM)
    names = ("pool_scale", "ln_g_rows", "ln_b_rows", "w_ssd", "w_cv", "w_pool", "w_out")
    return pl.pallas_call(
        functools.partial(_merge_body, cv_transposed=cv_transposed),
        out_shape=jax.ShapeDtypeStruct((n, d), F32),
        grid=(n // tm,),
        in_specs=[tok(d), tok(SSD_INNER), uc_spec, tok(POOL_PADDED), tok(3 * d), _mod_spec(l, 2, row_of_tile)]
                 + [_layer_spec(prm[k], l) for k in names],
        out_specs=tok(d),
        compiler_params=_params(1),
        name="merge",
    )(x, ys, uc, pp, gates, mods, *[prm[k] for k in names])


def _ffn_body(x_ref, g_ref, sh_ref, sc_ref, gt_ref, wg_ref, wu_ref, wd_ref, fg_ref, o_ref, *, final, hid_chunk):
    x = x_ref[...]
    hb = _norm_mod(x, g_ref[0], sh_ref[0, 0], sc_ref[0, 0]).astype(BF16)
    acc = jnp.zeros(x.shape, F32)
    for j in range(0, FFN_HIDDEN, hid_chunk):
        je = min(j + hid_chunk, FFN_HIDDEN)
        gate = _dot(hb, wg_ref[0, :, j:je])
        up = _dot(hb, wu_ref[0, :, j:je])
        acc = acc + _dot((_silu(gate) * up).astype(BF16), wd_ref[0, j:je, :])
    out = x + gt_ref[0, 0] * acc
    if final:
        ms = jnp.mean(out * out, axis=-1, keepdims=True)
        out = (out * lax.rsqrt(ms + EPS)) * fg_ref[...]
    o_ref[...] = out


def _ffn(x, l, norm_g, mods, row_of_tile, w_gate, w_up, w_down, final_g, final, tm):
    n, d = x.shape
    return pl.pallas_call(
        functools.partial(_ffn_body, final=final, hid_chunk=6 * MXU_DIM),
        out_shape=jax.ShapeDtypeStruct((n, d), F32),
        grid=(n // tm,),
        in_specs=[pl.BlockSpec((tm, d), lambda i: (i, 0)), _layer_spec(norm_g, l),
                  _mod_spec(l, 3, row_of_tile), _mod_spec(l, 4, row_of_tile), _mod_spec(l, 5, row_of_tile),
                  _layer_spec(w_gate, l), _layer_spec(w_up, l), _layer_spec(w_down, l),
                  pl.BlockSpec((1, d), lambda i: (0, 0), pipeline_mode=pl.Buffered(1))],
        out_specs=pl.BlockSpec((tm, d), lambda i: (i, 0)),
        compiler_params=_params(1),
        name="ffn",
    )(x, norm_g, mods, mods, mods, w_gate, w_up, w_down, final_g.reshape(1, d))


W_MAIN_COLS = dict(z=(0, SSD_INNER), xbc=(SSD_INNER, SSD_XBC), dt=(SSD_INNER + SSD_XBC, DT_PAD),
                   pool=(SSD_INNER + SSD_XBC + DT_PAD, POOL_PADDED),
                   gates=(SSD_INNER + SSD_XBC + DT_PAD + POOL_PADDED, 3 * D_MODEL))


def _split_w_body(w_ref, main_ref, cv_ref, cvt_ref):
    o = [int(v) for v in np.cumsum([0, SSD_INNER, SSD_XBC, 2 * SSD_HEADS, 2 * CONV_DIM, POOL_DIM, 3 * D_MODEL])]
    rb = w_ref.shape[2]
    rows = lambda a, b: w_ref[0, a:b, :]

    def padded(a, b, n):
        return jnp.concatenate([rows(a, b), jnp.zeros((n - (b - a), rb), F32)], axis=0)

    parts = [rows(o[0], o[2])]
    for k in range(2):
        parts.append(padded(o[2] + k * SSD_HEADS, o[2] + (k + 1) * SSD_HEADS, LANES))
    for k in range(POOL_GROUPS):
        parts.append(padded(o[4] + k * POOL_GROUP_DIM, o[4] + (k + 1) * POOL_GROUP_DIM, POOL_PAD))
    parts.append(0.5 * rows(o[5], o[6]))
    off = 0
    for p in parts:
        main_ref[0, :, off:off + p.shape[0]] = p.T.astype(BF16)
        off += p.shape[0]
    cv_t = rows(o[3], o[4])
    cvt_ref[0] = cv_t.astype(BF16)
    cv_ref[0] = cv_t.T.astype(BF16)


def _split_w_in(w):
    nl, d, n_in = w.shape
    rb = 256
    n_main = sum(c for _, c in W_MAIN_COLS.values())
    return pl.pallas_call(
        _split_w_body,
        out_shape=[jax.ShapeDtypeStruct((nl, d, n_main), BF16), jax.ShapeDtypeStruct((nl, d, 2 * CONV_DIM), BF16),
                   jax.ShapeDtypeStruct((nl, 2 * CONV_DIM, d), BF16)],
        grid=(nl, d // rb),
        in_specs=[pl.BlockSpec((1, n_in, rb), lambda l, i: (l, 0, i))],
        out_specs=[pl.BlockSpec((1, rb, n_main), lambda l, i: (l, i, 0)),
                   pl.BlockSpec((1, rb, 2 * CONV_DIM), lambda l, i: (l, i, 0)),
                   pl.BlockSpec((1, 2 * CONV_DIM, rb), lambda l, i: (l, 0, i))],
        compiler_params=_params(2),
        name="split_w_in",
    )(jnp.swapaxes(w, 1, 2))


def kernel(x, c, ctx, c_ctx, w_ada, b_ada, norm1_g, norm2_g, w_in, ssd_conv_w, ssd_conv_b, ssd_a_log,
           ssd_dt_bias, ssd_d, ssd_norm_g, ssd_w_out, cv_dw_w, cv_dw_b, cv_ln_g, cv_ln_b, cv_w_out, pool_w,
           pool_scale, w_out, ffn_w_gate, ffn_w_up, ffn_w_down, final_g):
    nb, t, d = x.shape
    tc = ctx.shape[1]
    depth = w_in.shape[0]
    rows = t // GRID_W

    n_cond = -(-(nb + 1) // 8) * 8
    cc = jnp.concatenate([c, c_ctx[None, :], jnp.zeros((n_cond - nb - 1, d), F32)], axis=0)
    mods = _ada(cc, w_ada, b_ada)

    tm_lat = min(512, t)
    tm_ctx = min(256, tc)
    ts_lat = min(1024, t)
    ts_ctx = min(256, tc)
    cb_lat = min(32, GRID_W)
    lat_row = lambda i: i // (t // tm_lat)
    ctx_row = lambda i: nb

    w_main, w_cv, w_cv_t = _split_w_in(w_in)
    seg = lambda name, arr=0, tr=False: (arr,) + W_MAIN_COLS[name] + (tr,)
    w_fg, w_fu, w_fd = (v.astype(BF16) for v in (ffn_w_gate, ffn_w_up, ffn_w_down))
    row3 = lambda v: v.reshape(depth, 1, v.shape[-1])
    norm1, norm2 = row3(norm1_g), row3(norm2_g)
    ln_rows = lambda v: jnp.broadcast_to(v[:, :, None], (depth, CONV_DIM, LANES))
    merge_prm = dict(
        pool_scale=row3(pool_scale), ln_g_rows=ln_rows(cv_ln_g), ln_b_rows=ln_rows(cv_ln_b),
        w_ssd=ssd_w_out.astype(BF16), w_cv=cv_w_out.astype(BF16),
        w_pool=jnp.pad(pool_w, ((0, 0), (0, 0), (0, POOL_PAD - POOL_GROUP_DIM), (0, 0))).astype(BF16),
        w_out=w_out.astype(BF16))
    ssd_prm = _ssd_params(ssd_a_log, ssd_dt_bias, ssd_d, ssd_norm_g)
    conv5 = (jnp.pad(ssd_conv_w, ((0, 0), (0, 8 - SSD_CONV), (0, 0))), row3(ssd_conv_b))
    cv_vpu_prm = (jnp.pad(cv_dw_w, ((0, 0), (0, 32 - CONV_WIDTH), (0, 0))), row3(cv_dw_b), row3(cv_ln_g), row3(cv_ln_b))
    toe = _toeplitz_pairs(cv_dw_w)
    cv_bias = jnp.broadcast_to(cv_dw_b.reshape(-1)[:, None, None], (depth * CONV_DIM, 1, LANES))

    xl = x.reshape(nb * t, d)
    xc = ctx.reshape(nb * tc, d)
    zero_state = jnp.zeros((nb, 2, SSD_STATE, SSD_INNER), F32)

    for l in range(depth):
        last = l == depth - 1

        if last:
            xbc_c, dt_c = _inproj(xc, l, norm1, mods, ctx_row, [w_main], [seg("xbc"), seg("dt")], [BF16, F32],
                                  tm_ctx, tc, 0, *conv5)
            z_c = jnp.zeros((nb, tc, SSD_INNER), BF16)
        else:
            z_c, xbc_c, dt_c, cv_c, pool_c, gates_c = _inproj(
                xc, l, norm1, mods, ctx_row, [w_main, w_cv],
                [seg("z"), seg("xbc"), seg("dt"), (1, 0, 2 * CONV_DIM, False), seg("pool"), seg("gates")],
                [BF16, BF16, F32, BF16, BF16, BF16], tm_ctx, tc, 1, *conv5)
            z_c = z_c.reshape(nb, tc, SSD_INNER)
        y_c, ctx_state = _ssd(xbc_c.reshape(nb, tc, SSD_XBC), dt_c.reshape(nb, tc, DT_PAD), z_c, l, ssd_prm,
                              zero_state, ts_ctx)
        if not last:
            uc_c = _conv_branch(cv_c, tc, l, *cv_vpu_prm)
            pp_c = _pool_seq(pool_c, tc)
            xc = _merge(xc, y_c.reshape(nb * tc, SSD_INNER), uc_c, pp_c, gates_c, l, mods, ctx_row, merge_prm,
                        tm_ctx, False)
            xc = _ffn(xc, l, norm2, mods, ctx_row, w_fg, w_fu, w_fd, final_g, False, tm_ctx)

        z_l, xbc_l, dt_l, cv_l, pool_l, gates_l = _inproj(
            xl, l, norm1, mods, lat_row, [w_main, w_cv_t],
            [seg("z"), seg("xbc"), seg("dt"), (1, 0, 2 * CONV_DIM, True), seg("pool"), seg("gates")],
            [BF16, BF16, F32, BF16, BF16, BF16], tm_lat, t, 1, *conv5)
        y_l, _ = _ssd(xbc_l.reshape(nb, t, SSD_XBC), dt_l.reshape(nb, t, DT_PAD), z_l.reshape(nb, t, SSD_INNER),
                      l, ssd_prm, ctx_state, ts_lat)
        v_l = _conv_branch_mxu(cv_l, l, toe, cv_bias)
        pp_l = _pool_branch(pool_l.reshape(nb, rows, GRID_W, POOL_PADDED), rows, cb_lat).reshape(-1, POOL_PADDED)
        xl = _merge(xl, y_l.reshape(nb * t, SSD_INNER), v_l, pp_l, gates_l, l, mods, lat_row, merge_prm,
                    tm_lat, True)
        xl = _ffn(xl, l, norm2, mods, lat_row, w_fg, w_fu, w_fd, final_g, last, tm_lat)

    return xl.reshape(nb, t, d)
```
